```python
import math
import jax, jax.numpy as jnp
from jax import lax
import numpy as np

D_MODEL = 2048
BATCH = 1
SEQ = 16384
DEPTH = 2
DEC_BATCH = 2
DEC_SEQ = 4096
PAST_LEN = 128

N_MIXERS = 2
N_POOL_LAYERS = (DEPTH + 1) // 2
N_ATTN_LAYERS = DEPTH // 2
PLE_DIM = 256
EPS = 1e-6

POOL_EXPAND = 2
POOL_WIDTH = POOL_EXPAND * D_MODEL
POOL_WINDOWS = (2, 4, 8, 16)
N_POOL_GROUPS = len(POOL_WINDOWS)
POOL_GROUP_WIDTH = POOL_WIDTH // N_POOL_GROUPS

HEAD_DIM = 128
N_Q_HEADS = D_MODEL // HEAD_DIM
N_KV_HEADS = 4
GQA_GROUP = N_Q_HEADS // N_KV_HEADS
Q_WIDTH = N_Q_HEADS * HEAD_DIM
KV_WIDTH = N_KV_HEADS * HEAD_DIM
ATTN_IN_WIDTH = 2 * Q_WIDTH + 2 * KV_WIDTH
WINDOW = 128
BLOCK = 128
ROPE_DIM = HEAD_DIM // 4
ROPE_THETA = 500000.0
NEG_BIG = -1e30

kernel_name = "hybrid_pool_swa_sink_encoder"


def rmsnorm(x, g):
    xf = x.astype(jnp.float32)
    r = lax.rsqrt(jnp.mean(xf * xf, axis=-1, keepdims=True) + EPS)
    return (xf * r * g.astype(jnp.float32)).astype(x.dtype)


def partial_rope(x, pos):
    half = ROPE_DIM // 2
    freq = ROPE_THETA ** (-jnp.arange(0, ROPE_DIM, 2, dtype=jnp.float32) / ROPE_DIM)
    ang = pos[:, None] * freq[None, :]
    cos = jnp.cos(ang)[None, :, None, :]
    sin = jnp.sin(ang)[None, :, None, :]
    xf = x.astype(jnp.float32)
    x1 = xf[..., :half]
    x2 = xf[..., half:ROPE_DIM]
    rot = jnp.concatenate([x1 * cos - x2 * sin, x2 * cos + x1 * sin, xf[..., ROPE_DIM:]], axis=-1)
    return rot.astype(x.dtype)


def pool_branch(h, w_in, w_grp, scale, w_out):
    B, S, _ = h.shape
    proj = h @ w_in
    u, z = proj[..., :POOL_WIDTH], proj[..., POOL_WIDTH:]
    uf = u.astype(jnp.float32).reshape(B, S, N_POOL_GROUPS, POOL_GROUP_WIDTH)
    cs = jnp.concatenate([jnp.zeros((B, 1, N_POOL_GROUPS, POOL_GROUP_WIDTH), jnp.float32),
                          jnp.cumsum(uf, axis=1)], axis=1)
    t = jnp.arange(S)
    mixed = []
    for gi, w in enumerate(POOL_WINDOWS):
        lo = jnp.clip(t - w // 2, 0, S)
        hi = jnp.clip(t - w // 2 + w, 0, S)
        cnt = (hi - lo).astype(jnp.float32)[None, :, None]
        csg = cs[:, :, gi, :]
        mean = (jnp.take(csg, hi, axis=1) - jnp.take(csg, lo, axis=1)) / cnt
        mixed.append(mean - uf[:, :, gi, :])
    m = jnp.stack(mixed, axis=2).astype(h.dtype)
    m = jnp.einsum('bsgc,gcd->bsgd', m, w_grp).reshape(B, S, POOL_WIDTH) * scale
    return (m * jax.nn.silu(z)) @ w_out


def banded_attention(q, k, v, sink):
    B, S, _, _ = q.shape
    nb = S // BLOCK
    qb = q.reshape(B, nb, BLOCK, N_KV_HEADS, GQA_GROUP, HEAD_DIM)
    pad = ((0, 0), (BLOCK, BLOCK), (0, 0), (0, 0))
    kp = jnp.pad(k, pad).reshape(B, nb + 2, BLOCK, N_KV_HEADS, HEAD_DIM)
    vp = jnp.pad(v, pad).reshape(B, nb + 2, BLOCK, N_KV_HEADS, HEAD_DIM)
    kw = jnp.concatenate([kp[:, :-2], kp[:, 1:-1], kp[:, 2:]], axis=2)
    vw = jnp.concatenate([vp[:, :-2], vp[:, 1:-1], vp[:, 2:]], axis=2)
    s = jnp.einsum('bnqhgd,bnkhd->bnhgqk', qb, kw,
                   preferred_element_type=jnp.float32) * (1.0 / math.sqrt(HEAD_DIM))
    blk = jnp.arange(nb)
    qpos = blk[:, None] * BLOCK + jnp.arange(BLOCK)[None, :]
    kpos = (blk[:, None] - 1) * BLOCK + jnp.arange(3 * BLOCK)[None, :]
    rel = kpos[:, None, :] - qpos[:, :, None]
    valid = (jnp.abs(rel) <= WINDOW) & (kpos[:, None, :] >= 0) & (kpos[:, None, :] < S)
    s = jnp.where(valid[None, :, None, None], s, NEG_BIG)
    sk = sink.astype(jnp.float32).reshape(N_KV_HEADS, GQA_GROUP)[None, None, :, :, None, None]
    m = jnp.maximum(jnp.max(s, axis=-1, keepdims=True), sk)
    e = jnp.exp(s - m)
    pr = e / (jnp.sum(e, axis=-1, keepdims=True) + jnp.exp(sk - m))
    o = jnp.einsum('bnhgqk,bnkhd->bnqhgd', pr.astype(v.dtype), vw)
    return o.reshape(B, S, N_Q_HEADS, HEAD_DIM)


def attn_branch(h, w_in, q_norm_g, k_norm_g, sink, w_out):
    B, S, _ = h.shape
    proj = h @ w_in
    q = proj[..., :Q_WIDTH].reshape(B, S, N_Q_HEADS, HEAD_DIM)
    k = proj[..., Q_WIDTH:Q_WIDTH + KV_WIDTH].reshape(B, S, N_KV_HEADS, HEAD_DIM)
    v = proj[..., Q_WIDTH + KV_WIDTH:Q_WIDTH + 2 * KV_WIDTH].reshape(B, S, N_KV_HEADS, HEAD_DIM)
    z = proj[..., Q_WIDTH + 2 * KV_WIDTH:]
    q = rmsnorm(q, q_norm_g)
    k = rmsnorm(k, k_norm_g)
    pos = jnp.arange(S, dtype=jnp.float32)
    q = partial_rope(q, pos)
    k = partial_rope(k, pos)
    o = banded_attention(q, k, v, sink).reshape(B, S, Q_WIDTH)
    return (o * jax.nn.silu(z)) @ w_out


def trunk(x, p, norm_g, a_w_in, a_w_grp, a_scale, a_w_out,
          b_w_in, b_q_norm, b_k_norm, b_sink, b_w_out,
          pe_w_proj, pe_norm_g, pe_w_gate):
    for i in range(DEPTH):
        hn = rmsnorm(x, norm_g[i])
        j = i // N_MIXERS
        if i % N_MIXERS == 0:
            y = pool_branch(hn, a_w_in[j], a_w_grp[j], a_scale[j], a_w_out[j])
        else:
            y = attn_branch(hn, b_w_in[j], b_q_norm[j], b_k_norm[j], b_sink[j], b_w_out[j])
        h = x + y
        e = p[i] @ pe_w_proj[i]
        gate = jax.nn.sigmoid(rmsnorm(h, pe_norm_g[i]) @ pe_w_gate[i])
        x = h + e * gate
    return x


def setup_inputs(seed: int = 0) -> dict:
    key = jax.random.key(seed)
    ks = jax.random.split(key, 20)
    f32 = jnp.float32
    nrm = lambda k, shp, sc: jax.random.normal(k, shp, f32) * sc
    return {
        "x_prompt": nrm(ks[0], (BATCH, SEQ, D_MODEL), 1.0),
        "x_sample": nrm(ks[1], (DEC_BATCH, DEC_SEQ, D_MODEL), 1.0),
        "p_prompt": nrm(ks[2], (DEPTH, BATCH, SEQ, PLE_DIM), 1.0),
        "p_sample": nrm(ks[3], (DEPTH, DEC_BATCH, DEC_SEQ, PLE_DIM), 1.0),
        "norm_g": 1.0 + nrm(ks[4], (DEPTH, D_MODEL), 0.05),
        "a_w_in": nrm(ks[5], (N_POOL_LAYERS, D_MODEL, 2 * POOL_WIDTH), D_MODEL ** -0.5),
        "a_w_grp": nrm(ks[6], (N_POOL_LAYERS, N_POOL_GROUPS, POOL_GROUP_WIDTH, POOL_GROUP_WIDTH),
                        POOL_GROUP_WIDTH ** -0.5),
        "a_scale": 1.0 + nrm(ks[7], (N_POOL_LAYERS, POOL_WIDTH), 0.1),
        "a_w_out": nrm(ks[8], (N_POOL_LAYERS, POOL_WIDTH, D_MODEL), POOL_WIDTH ** -0.5),
        "b_w_in": nrm(ks[9], (N_ATTN_LAYERS, D_MODEL, ATTN_IN_WIDTH), D_MODEL ** -0.5),
        "b_q_norm": 1.0 + nrm(ks[10], (N_ATTN_LAYERS, HEAD_DIM), 0.05),
        "b_k_norm": 1.0 + nrm(ks[11], (N_ATTN_LAYERS, HEAD_DIM), 0.05),
        "b_sink": nrm(ks[12], (N_ATTN_LAYERS, N_Q_HEADS), 0.5),
        "b_w_out": nrm(ks[13], (N_ATTN_LAYERS, Q_WIDTH, D_MODEL), Q_WIDTH ** -0.5),
        "pe_w_proj": nrm(ks[14], (DEPTH, PLE_DIM, D_MODEL), PLE_DIM ** -0.5),
        "pe_norm_g": 1.0 + nrm(ks[15], (DEPTH, D_MODEL), 0.05),
        "pe_w_gate": nrm(ks[16], (DEPTH, D_MODEL, D_MODEL), D_MODEL ** -0.5),
    }


def reference(x_prompt, x_sample, p_prompt, p_sample, norm_g, a_w_in, a_w_grp, a_scale, a_w_out,
              b_w_in, b_q_norm, b_k_norm, b_sink, b_w_out, pe_w_proj, pe_norm_g, pe_w_gate):
    y_prompt = trunk(x_prompt, p_prompt, norm_g, a_w_in, a_w_grp, a_scale, a_w_out,
                     b_w_in, b_q_norm, b_k_norm, b_sink, b_w_out, pe_w_proj, pe_norm_g, pe_w_gate)
    y_sample = trunk(x_sample, p_sample, norm_g, a_w_in, a_w_grp, a_scale, a_w_out,
                     b_w_in, b_q_norm, b_k_norm, b_sink, b_w_out, pe_w_proj, pe_norm_g, pe_w_gate)
    return (y_prompt, y_sample)
```

```python
import functools
import math

import jax
import jax.numpy as jnp
from jax import lax
from jax.experimental import pallas as pl
from jax.experimental.pallas import tpu as pltpu

F32 = jnp.float32
BF16 = jnp.bfloat16

D_MODEL = 2048
PLE_DIM = 256
EPS = 1e-6

POOL_WIDTH = 4096
POOL_WINDOWS = (2, 4, 8, 16)
N_POOL_GROUPS = len(POOL_WINDOWS)
POOL_GROUP_WIDTH = POOL_WIDTH // N_POOL_GROUPS
POOL_HALO = 8

HEAD_DIM = 128
N_Q_HEADS = 16
N_KV_HEADS = 4
GQA_GROUP = N_Q_HEADS // N_KV_HEADS
Q_WIDTH = N_Q_HEADS * HEAD_DIM
KV_WIDTH = N_KV_HEADS * HEAD_DIM
ATTN_IN_WIDTH = 2 * Q_WIDTH + 2 * KV_WIDTH
ATTN_BLOCK = 128
ROPE_DIM = HEAD_DIM // 4
ROPE_THETA = 500000.0
NEG_BIG = -1e30

VMEM_LIMIT_BYTES = 56 * 1024 * 1024
LANES = 128
SUBLANES = 8

POOL_TM = 512
ATTN_FRONT_TM = 512
ATTN_FRONT_TN = 1024
ATTN_CORE_TM = 512
OUT_TM = 256


def _rms_scale(x):
    return lax.rsqrt(jnp.mean(x * x, axis=-1, keepdims=True) + EPS)


def _dot(a, b):
    return jnp.dot(a, b, preferred_element_type=F32)


def _silu(z):
    return z * jax.nn.sigmoid(z)


def _pool_front_kernel(x_ref, xprev_ref, xnext_ref, gain_ref, wu_ref, wz_ref, wgrp_ref,
                       scale_ref, t_ref, hn_ref, a_ref, ext_ref, *, tm, seq):
    i = pl.program_id(0)
    g = pl.program_id(1)
    n_ext = tm + 2 * POOL_HALO

    @pl.when(g == 0)
    def _():
        t0 = lax.rem(i * tm, seq)
        has_prev = t0 > 0
        has_next = t0 + tm < seq
        x = x_ref[...]
        r = _rms_scale(x)
        xp = xprev_ref[...]
        rp = jnp.where(has_prev, _rms_scale(xp), 0.0)
        xn = xnext_ref[...]
        rn = jnp.where(has_next, _rms_scale(xn), 0.0)
        pos = t0 + lax.broadcasted_iota(jnp.int32, (tm, 1), 0)
        inv_cnt = []
        for win in POOL_WINDOWS:
            half = win // 2
            cnt = jnp.minimum(pos + half, seq) - jnp.maximum(pos - half, 0)
            inv_cnt.append(1.0 / cnt.astype(F32))

        def col_chunk(c, carry):
            cols = pl.ds(pl.multiple_of(c * LANES, LANES), LANES)
            gain = gain_ref[:, cols]
            hc = x_ref[:, cols] * r * gain
            hn_ref[:, cols] = hc.astype(BF16)
            ext_ref[0:POOL_HALO, :] = xprev_ref[:, cols] * rp * gain
            ext_ref[POOL_HALO:POOL_HALO + tm, :] = hc
            ext_ref[POOL_HALO + tm:n_ext, :] = xnext_ref[:, cols] * rn * gain
            ext = ext_ref[...]
            w = ext + pltpu.roll(ext, 1, axis=0)
            for gi, win in enumerate(POOL_WINDOWS):
                if gi > 0:
                    s = win // 4
                    w = pltpu.roll(w, s, axis=0) + pltpu.roll(w, n_ext - s, axis=0)
                pooled = w[POOL_HALO:POOL_HALO + tm, :] * inv_cnt[gi]
                a_ref[gi, :, cols] = (pooled - hc).astype(BF16)
            return carry

        lax.fori_loop(0, D_MODEL // LANES, col_chunk, 0)

    m = _dot(a_ref[g], wu_ref[...])
    z = _dot(hn_ref[...], wz_ref[...])
    mm = _dot(m.astype(BF16), wgrp_ref[0])
    t_ref[...] = ((mm * scale_ref[...]) * _silu(z)).astype(BF16)


def _pool_front(x2d, gain, w_in, w_grp, scale, *, seq):
    rows = x2d.shape[0]
    tm = POOL_TM
    c = POOL_GROUP_WIDTH
    n_row_blocks8 = rows // SUBLANES
    kernel = functools.partial(_pool_front_kernel, tm=tm, seq=seq)
    return pl.pallas_call(
        kernel,
        name="pool_front",
        grid=(rows // tm, N_POOL_GROUPS),
        in_specs=[
            pl.BlockSpec((tm, D_MODEL), lambda i, g: (i, 0)),
            pl.BlockSpec((POOL_HALO, D_MODEL),
                         lambda i, g: (jnp.maximum(i * (tm // SUBLANES) - 1, 0), 0)),
            pl.BlockSpec((POOL_HALO, D_MODEL),
                         lambda i, g: (jnp.minimum((i + 1) * (tm // SUBLANES), n_row_blocks8 - 1), 0)),
            pl.BlockSpec((1, D_MODEL), lambda i, g: (0, 0)),
            pl.BlockSpec((D_MODEL, c), lambda i, g: (0, g)),
            pl.BlockSpec((D_MODEL, c), lambda i, g: (0, N_POOL_GROUPS + g)),
            pl.BlockSpec((1, c, c), lambda i, g: (g, 0, 0)),
            pl.BlockSpec((1, c), lambda i, g: (0, g)),
        ],
        out_specs=pl.BlockSpec((tm, c), lambda i, g: (i, g)),
        out_shape=jax.ShapeDtypeStruct((rows, POOL_WIDTH), BF16),
        scratch_shapes=[
            pltpu.VMEM((tm, D_MODEL), BF16),
            pltpu.VMEM((N_POOL_GROUPS, tm, D_MODEL), BF16),
            pltpu.VMEM((tm + 2 * POOL_HALO, LANES), F32),
        ],
        compiler_params=pltpu.CompilerParams(
            dimension_semantics=("arbitrary", "arbitrary"),
            vmem_limit_bytes=VMEM_LIMIT_BYTES),
    )(x2d, x2d, x2d, gain, w_in, w_in, w_grp, scale)


def _out_pe_kernel(t_ref, wout_ref, x_ref, p_ref, wp_ref, gain_ref, wgate_ref, out_ref):
    h = x_ref[...] + _dot(t_ref[...], wout_ref[...])
    hn = (h * _rms_scale(h) * gain_ref[...]).astype(BF16)
    gate = jax.nn.sigmoid(_dot(hn, wgate_ref[...]))
    e = _dot(p_ref[...].astype(BF16), wp_ref[...])
    out_ref[...] = h + e * gate


def _out_pe(t2d, w_out, x2d, p2d, p_block_offset, w_proj, gain, w_gate):
    rows = x2d.shape[0]
    k = t2d.shape[1]
    tm = OUT_TM
    resident = pl.Buffered(1)
    return pl.pallas_call(
        _out_pe_kernel,
        name="out_pe",
        grid=(rows // tm,),
        in_specs=[
            pl.BlockSpec((tm, k), lambda i: (i, 0)),
            pl.BlockSpec((k, D_MODEL), lambda i: (0, 0), pipeline_mode=resident),
            pl.BlockSpec((tm, D_MODEL), lambda i: (i, 0)),
            pl.BlockSpec((tm, PLE_DIM), lambda i: (i + p_block_offset, 0)),
            pl.BlockSpec((PLE_DIM, D_MODEL), lambda i: (0, 0), pipeline_mode=resident),
            pl.BlockSpec((1, D_MODEL), lambda i: (0, 0)),
            pl.BlockSpec((D_MODEL, D_MODEL), lambda i: (0, 0), pipeline_mode=resident),
        ],
        out_specs=pl.BlockSpec((tm, D_MODEL), lambda i: (i, 0)),
        out_shape=jax.ShapeDtypeStruct((rows, D_MODEL), F32),
        compiler_params=pltpu.CompilerParams(
            dimension_semantics=("arbitrary",),
            vmem_limit_bytes=VMEM_LIMIT_BYTES),
    )(t2d, w_out, x2d, p2d, w_proj, gain, w_gate)


def _attn_front_kernel(x_ref, gain_ref, w_ref, cos_ref, sin_lo_ref, sin_hi_ref, qg_ref, kg_ref,
                       q_ref, k_ref, v_ref, z_ref, hn_ref):
    n = pl.program_id(1)

    @pl.when(n == 0)
    def _():
        x = x_ref[...]
        hn_ref[...] = (x * _rms_scale(x) * gain_ref[...]).astype(BF16)

    y = _dot(hn_ref[...], w_ref[...])

    def norm_rope(slab, head_gain):
        s = slab * _rms_scale(slab) * head_gain
        return (s * cos_ref[...]
                + pltpu.roll(s, HEAD_DIM - ROPE_DIM // 2, axis=1) * sin_lo_ref[...]
                + pltpu.roll(s, ROPE_DIM // 2, axis=1) * sin_hi_ref[...])

    heads_per_block = ATTN_FRONT_TN // HEAD_DIM

    @pl.when(n < Q_WIDTH // ATTN_FRONT_TN)
    def _():
        for h in range(heads_per_block):
            cols = slice(h * HEAD_DIM, (h + 1) * HEAD_DIM)
            q_ref[:, cols] = norm_rope(y[:, cols], qg_ref[...]).astype(BF16)

    @pl.when(n == Q_WIDTH // ATTN_FRONT_TN)
    def _():
        for h in range(N_KV_HEADS):
            cols = slice(h * HEAD_DIM, (h + 1) * HEAD_DIM)
            k_ref[:, cols] = norm_rope(y[:, cols], kg_ref[...]).astype(BF16)
        v_ref[...] = y[:, KV_WIDTH:].astype(BF16)

    @pl.when(n > Q_WIDTH // ATTN_FRONT_TN)
    def _():
        z_ref[...] = y


def _attn_front(x2d, gain, w_in, cos_t, sin_lo_t, sin_hi_t, q_gain, k_gain, *, seq):
    rows = x2d.shape[0]
    tm = ATTN_FRONT_TM
    tn = ATTN_FRONT_TN
    assert 2 * KV_WIDTH == tn
    n_q_blocks = Q_WIDTH // tn
    seq_blocks = seq // tm
    table_spec = pl.BlockSpec((tm, HEAD_DIM), lambda i, n: (lax.rem(i, seq_blocks), 0))
    return pl.pallas_call(
        _attn_front_kernel,
        name="attn_front",
        grid=(rows // tm, ATTN_IN_WIDTH // tn),
        in_specs=[
            pl.BlockSpec((tm, D_MODEL), lambda i, n: (i, 0)),
            pl.BlockSpec((1, D_MODEL), lambda i, n: (0, 0)),
            pl.BlockSpec((D_MODEL, tn), lambda i, n: (0, n)),
            table_spec, table_spec, table_spec,
            pl.BlockSpec((1, HEAD_DIM), lambda i, n: (0, 0)),
            pl.BlockSpec((1, HEAD_DIM), lambda i, n: (0, 0)),
        ],
        out_specs=[
            pl.BlockSpec((tm, tn), lambda i, n: (i, jnp.minimum(n, n_q_blocks - 1))),
            pl.BlockSpec((tm, KV_WIDTH), lambda i, n: (i, 0)),
            pl.BlockSpec((tm, KV_WIDTH), lambda i, n: (i, 0)),
            pl.BlockSpec((tm, tn), lambda i, n: (i, jnp.maximum(n - n_q_blocks - 1, 0))),
        ],
        out_shape=[
            jax.ShapeDtypeStruct((rows, Q_WIDTH), BF16),
            jax.ShapeDtypeStruct((rows, KV_WIDTH), BF16),
            jax.ShapeDtypeStruct((rows, KV_WIDTH), BF16),
            jax.ShapeDtypeStruct((rows, Q_WIDTH), F32),
        ],
        scratch_shapes=[pltpu.VMEM((tm, D_MODEL), BF16)],
        compiler_params=pltpu.CompilerParams(
            dimension_semantics=("arbitrary", "arbitrary"),
            vmem_limit_bytes=VMEM_LIMIT_BYTES),
    )(x2d, gain, w_in, cos_t, sin_lo_t, sin_hi_t, q_gain, k_gain)


def _attn_core_kernel(sink_ref, q_ref, kprev_ref, kmid_ref, knext_ref, vprev_ref, vmid_ref,
                      vnext_ref, z_ref, o_ref, kbuf_ref, vbuf_ref, *, tm, seq):
    i = pl.program_id(0)
    blk = ATTN_BLOCK
    t0 = lax.rem(i * tm, seq)
    kbuf_ref[0:blk, :] = kprev_ref[...]
    kbuf_ref[blk:blk + tm, :] = kmid_ref[...]
    kbuf_ref[blk + tm:, :] = knext_ref[...]
    vbuf_ref[0:blk, :] = vprev_ref[...]
    vbuf_ref[blk:blk + tm, :] = vmid_ref[...]
    vbuf_ref[blk + tm:, :] = vnext_ref[...]

    n_rows = GQA_GROUP * blk
    qi = lax.rem(lax.broadcasted_iota(jnp.int32, (n_rows, 3 * blk), 0), blk)
    kj = lax.broadcasted_iota(jnp.int32, (n_rows, 3 * blk), 1)
    rel = kj - blk - qi
    in_band = jnp.abs(rel) <= ATTN_BLOCK
    inv_sqrt_d = 1.0 / math.sqrt(HEAD_DIM)

    def q_block(qb, carry):
        row0 = pl.multiple_of(qb * blk, blk)
        rows = pl.ds(row0, blk)
        kpos = t0 + row0 - blk + kj
        valid = in_band & (kpos >= 0) & (kpos < seq)
        for kvh in range(N_KV_HEADS):
            kcols = slice(kvh * HEAD_DIM, (kvh + 1) * HEAD_DIM)
            heads = [kvh * GQA_GROUP + hg for hg in range(GQA_GROUP)]
            qs = jnp.concatenate(
                [q_ref[rows, h * HEAD_DIM:(h + 1) * HEAD_DIM] for h in heads], axis=0)
            kw = kbuf_ref[pl.ds(row0, 3 * blk), kcols]
            vw = vbuf_ref[pl.ds(row0, 3 * blk), kcols]
            s = lax.dot_general(qs, kw, (((1,), (1,)), ((), ())),
                                preferred_element_type=F32) * inv_sqrt_d
            s = jnp.where(valid, s, NEG_BIG)
            sk = jnp.concatenate(
                [jnp.full((blk, 1), sink_ref[h], F32) for h in heads], axis=0)
            m = jnp.maximum(jnp.max(s, axis=-1, keepdims=True), sk)
            e = jnp.exp(s - m)
            denom = jnp.sum(e, axis=-1, keepdims=True) + jnp.exp(sk - m)
            o = _dot(e.astype(BF16), vw) * (1.0 / denom)
            for hg, h in enumerate(heads):
                hcols = slice(h * HEAD_DIM, (h + 1) * HEAD_DIM)
                o_ref[rows, hcols] = (o[hg * blk:(hg + 1) * blk] * _silu(z_ref[rows, hcols])
                                      ).astype(BF16)
        return carry

    lax.fori_loop(0, tm // blk, q_block, 0)


def _attn_core(sink, q, k, v, z, *, seq):
    rows = q.shape[0]
    tm = ATTN_CORE_TM
    blk = ATTN_BLOCK
    per = tm // blk
    n_blocks = rows // blk
    prev_map = lambda i: (jnp.maximum(i * per - 1, 0), 0)
    next_map = lambda i: (jnp.minimum((i + 1) * per, n_blocks - 1), 0)
    mid_map = lambda i: (i, 0)
    kernel = functools.partial(_attn_core_kernel, tm=tm, seq=seq)
    return pl.pallas_call(
        kernel,
        name="attn_core",
        grid=(rows // tm,),
        in_specs=[
            pl.BlockSpec(memory_space=pltpu.SMEM),
            pl.BlockSpec((tm, Q_WIDTH), mid_map),
            pl.BlockSpec((blk, KV_WIDTH), prev_map),
            pl.BlockSpec((tm, KV_WIDTH), mid_map),
            pl.BlockSpec((blk, KV_WIDTH), next_map),
            pl.BlockSpec((blk, KV_WIDTH), prev_map),
            pl.BlockSpec((tm, KV_WIDTH), mid_map),
            pl.BlockSpec((blk, KV_WIDTH), next_map),
            pl.BlockSpec((tm, Q_WIDTH), mid_map),
        ],
        out_specs=pl.BlockSpec((tm, Q_WIDTH), mid_map),
        out_shape=jax.ShapeDtypeStruct((rows, Q_WIDTH), BF16),
        scratch_shapes=[
            pltpu.VMEM((tm + 2 * blk, KV_WIDTH), BF16),
            pltpu.VMEM((tm + 2 * blk, KV_WIDTH), BF16),
        ],
        compiler_params=pltpu.CompilerParams(
            dimension_semantics=("arbitrary",),
            vmem_limit_bytes=VMEM_LIMIT_BYTES),
    )(sink, q, k, k, k, v, v, v, z)


def _rope_tables(seq):
    half = ROPE_DIM // 2
    freq = ROPE_THETA ** (-jnp.arange(0, ROPE_DIM, 2, dtype=F32) / ROPE_DIM)
    pos = jnp.arange(seq, dtype=F32)
    ang = pos[:, None] * freq[None, :]
    cos = jnp.cos(ang)
    sin = jnp.sin(ang)
    rest = HEAD_DIM - ROPE_DIM
    cos_t = jnp.concatenate([cos, cos, jnp.ones((seq, rest), F32)], axis=1)
    sin_lo_t = jnp.concatenate([-sin, jnp.zeros((seq, half + rest), F32)], axis=1)
    sin_hi_t = jnp.concatenate([jnp.zeros((seq, half), F32), sin, jnp.zeros((seq, rest), F32)], axis=1)
    return cos_t, sin_lo_t, sin_hi_t


def _trunk(x, p, w, *, seq):
    batch = x.shape[0]
    rows = batch * seq
    x2d = x.reshape(rows, D_MODEL)
    p2d = p.reshape(p.shape[0] * rows, PLE_DIM)
    p_blocks_per_layer = rows // OUT_TM

    t = _pool_front(x2d, w["norm_g"][0:1], w["a_w_in"], w["a_w_grp"], w["a_scale"], seq=seq)
    x2d = _out_pe(t, w["a_w_out"], x2d, p2d, 0, w["pe_w_proj"][0], w["pe_norm_g"][0:1],
                  w["pe_w_gate"][0])

    cos_t, sin_lo_t, sin_hi_t = _rope_tables(seq)
    q, k, v, z = _attn_front(x2d, w["norm_g"][1:2], w["b_w_in"], cos_t, sin_lo_t, sin_hi_t,
                             w["b_q_norm"], w["b_k_norm"], seq=seq)
    og = _attn_core(w["b_sink"], q, k, v, z, seq=seq)
    x2d = _out_pe(og, w["b_w_out"], x2d, p2d, p_blocks_per_layer, w["pe_w_proj"][1],
                  w["pe_norm_g"][1:2], w["pe_w_gate"][1])
    return x2d.reshape(batch, seq, D_MODEL)


def kernel(x_prompt, x_sample, p_prompt, p_sample, norm_g, a_w_in, a_w_grp, a_scale, a_w_out,
           b_w_in, b_q_norm, b_k_norm, b_sink, b_w_out, pe_w_proj, pe_norm_g, pe_w_gate):
    w = {
        "norm_g": norm_g,
        "a_w_in": a_w_in[0].astype(BF16),
        "a_w_grp": a_w_grp[0].astype(BF16),
        "a_scale": a_scale,
        "a_w_out": a_w_out[0].astype(BF16),
        "b_w_in": b_w_in[0].astype(BF16),
        "b_q_norm": b_q_norm,
        "b_k_norm": b_k_norm,
        "b_sink": b_sink[0],
        "b_w_out": b_w_out[0].astype(BF16),
        "pe_w_proj": pe_w_proj.astype(BF16),
        "pe_norm_g": pe_norm_g,
        "pe_w_gate": pe_w_gate.astype(BF16),
    }
    y_prompt = _trunk(x_prompt, p_prompt, w, seq=x_prompt.shape[1])
    y_sample = _trunk(x_sample, p_sample, w, seq=x_sample.shape[1])
    return (y_prompt, y_sample)
```

```python
import functools
import math

import jax
import jax.numpy as jnp
from jax import lax
from jax.experimental import pallas as pl
from jax.experimental.pallas import tpu as pltpu

F32 = jnp.float32
BF16 = jnp.bfloat16

D_MODEL = 2048
PLE_DIM = 256
EPS = 1e-6

POOL_WIDTH = 4096
POOL_WINDOWS = (2, 4, 8, 16)
N_POOL_GROUPS = len(POOL_WINDOWS)
POOL_GROUP_WIDTH = POOL_WIDTH // N_POOL_GROUPS
POOL_HALO = 8

HEAD_DIM = 128
N_Q_HEADS = 16
N_KV_HEADS = 4
GQA_GROUP = N_Q_HEADS // N_KV_HEADS
Q_WIDTH = N_Q_HEADS * HEAD_DIM
KV_WIDTH = N_KV_HEADS * HEAD_DIM
ATTN_IN_WIDTH = 2 * Q_WIDTH + 2 * KV_WIDTH
ATTN_BLOCK = 128
ROPE_DIM = HEAD_DIM // 4
ROPE_THETA = 500000.0
NEG_BIG = -1e30
LOG2_E = math.log2(math.e)

VMEM_LIMIT_BYTES = 56 * 1024 * 1024
LANES = 128
SUBLANES = 8

POOL_TM = 512
ATTN_FRONT_TM = 512
ATTN_FRONT_TN = 1024
ATTN_CORE_TM = 512
OUT_TM = 256


def _rms_scale(x):
    return lax.rsqrt(jnp.mean(x * x, axis=-1, keepdims=True) + EPS)


def _dot(a, b):
    return jnp.dot(a, b, preferred_element_type=F32)


def _silu(z):
    return z * jax.nn.sigmoid(z)


def _pool_front_kernel(x_ref, xprev_ref, xnext_ref, gain_ref, wu_ref, wz_ref, wgrp_ref,
                       scale_ref, t_ref, hn_ref, a_ref, ext_ref, *, tm, seq):
    i = pl.program_id(0)
    g = pl.program_id(1)
    n_ext = tm + 2 * POOL_HALO

    @pl.when(g == 0)
    def _():
        t0 = lax.rem(i * tm, seq)
        has_prev = t0 > 0
        has_next = t0 + tm < seq
        x = x_ref[...]
        r = _rms_scale(x)
        xp = xprev_ref[...]
        rp = jnp.where(has_prev, _rms_scale(xp), 0.0)
        xn = xnext_ref[...]
        rn = jnp.where(has_next, _rms_scale(xn), 0.0)
        pos = t0 + lax.broadcasted_iota(jnp.int32, (tm, 1), 0)
        inv_cnt = []
        for win in POOL_WINDOWS:
            half = win // 2
            cnt = jnp.minimum(pos + half, seq) - jnp.maximum(pos - half, 0)
            inv_cnt.append(1.0 / cnt.astype(F32))

        def col_chunk(c, carry):
            cols = pl.ds(pl.multiple_of(c * LANES, LANES), LANES)
            gain = gain_ref[:, cols]
            hc = x_ref[:, cols] * r * gain
            hn_ref[:, cols] = hc.astype(BF16)
            ext_ref[0:POOL_HALO, :] = xprev_ref[:, cols] * rp * gain
            ext_ref[POOL_HALO:POOL_HALO + tm, :] = hc
            ext_ref[POOL_HALO + tm:n_ext, :] = xnext_ref[:, cols] * rn * gain
            ext = ext_ref[...]
            w = ext + pltpu.roll(ext, 1, axis=0)
            for gi, win in enumerate(POOL_WINDOWS):
                if gi > 0:
                    s = win // 4
                    w = pltpu.roll(w, s, axis=0) + pltpu.roll(w, n_ext - s, axis=0)
                pooled = w[POOL_HALO:POOL_HALO + tm, :] * inv_cnt[gi]
                a_ref[gi, :, cols] = (pooled - hc).astype(BF16)
            return carry

        lax.fori_loop(0, D_MODEL // LANES, col_chunk, 0)

    m = _dot(a_ref[g], wu_ref[...])
    z = _dot(hn_ref[...], wz_ref[...])
    mm = _dot(m.astype(BF16), wgrp_ref[0])
    t_ref[...] = ((mm * scale_ref[...]) * _silu(z)).astype(BF16)


def _pool_front(x2d, gain, w_in, w_grp, scale, *, seq):
    rows = x2d.shape[0]
    tm = POOL_TM
    c = POOL_GROUP_WIDTH
    n_row_blocks8 = rows // SUBLANES
    kernel = functools.partial(_pool_front_kernel, tm=tm, seq=seq)
    return pl.pallas_call(
        kernel,
        name="pool_front",
        grid=(rows // tm, N_POOL_GROUPS),
        in_specs=[
            pl.BlockSpec((tm, D_MODEL), lambda i, g: (i, 0)),
            pl.BlockSpec((POOL_HALO, D_MODEL),
                         lambda i, g: (jnp.maximum(i * (tm // SUBLANES) - 1, 0), 0)),
            pl.BlockSpec((POOL_HALO, D_MODEL),
                         lambda i, g: (jnp.minimum((i + 1) * (tm // SUBLANES), n_row_blocks8 - 1), 0)),
            pl.BlockSpec((1, D_MODEL), lambda i, g: (0, 0)),
            pl.BlockSpec((D_MODEL, c), lambda i, g: (0, g)),
            pl.BlockSpec((D_MODEL, c), lambda i, g: (0, N_POOL_GROUPS + g)),
            pl.BlockSpec((1, c, c), lambda i, g: (g, 0, 0)),
            pl.BlockSpec((1, c), lambda i, g: (0, g)),
        ],
        out_specs=pl.BlockSpec((tm, c), lambda i, g: (i, g)),
        out_shape=jax.ShapeDtypeStruct((rows, POOL_WIDTH), BF16),
        scratch_shapes=[
            pltpu.VMEM((tm, D_MODEL), BF16),
            pltpu.VMEM((N_POOL_GROUPS, tm, D_MODEL), BF16),
            pltpu.VMEM((tm + 2 * POOL_HALO, LANES), F32),
        ],
        compiler_params=pltpu.CompilerParams(
            dimension_semantics=("arbitrary", "arbitrary"),
            vmem_limit_bytes=VMEM_LIMIT_BYTES),
    )(x2d, x2d, x2d, gain, w_in, w_in, w_grp, scale)


def _out_pe_kernel(t_ref, wout_ref, x_ref, p_ref, wp_ref, gain_ref, wgate_ref, out_ref):
    h = x_ref[...] + _dot(t_ref[...], wout_ref[...])
    hn = (h * _rms_scale(h) * gain_ref[...]).astype(BF16)
    gate = jax.nn.sigmoid(_dot(hn, wgate_ref[...]))
    e = _dot(p_ref[...].astype(BF16), wp_ref[...])
    out_ref[...] = h + e * gate


def _out_pe(t2d, w_out, x2d, p2d, p_block_offset, w_proj, gain, w_gate):
    rows = x2d.shape[0]
    k = t2d.shape[1]
    tm = OUT_TM
    resident = pl.Buffered(1)
    return pl.pallas_call(
        _out_pe_kernel,
        name="out_pe",
        grid=(rows // tm,),
        in_specs=[
            pl.BlockSpec((tm, k), lambda i: (i, 0)),
            pl.BlockSpec((k, D_MODEL), lambda i: (0, 0), pipeline_mode=resident),
            pl.BlockSpec((tm, D_MODEL), lambda i: (i, 0)),
            pl.BlockSpec((tm, PLE_DIM), lambda i: (i + p_block_offset, 0)),
            pl.BlockSpec((PLE_DIM, D_MODEL), lambda i: (0, 0), pipeline_mode=resident),
            pl.BlockSpec((1, D_MODEL), lambda i: (0, 0)),
            pl.BlockSpec((D_MODEL, D_MODEL), lambda i: (0, 0), pipeline_mode=resident),
        ],
        out_specs=pl.BlockSpec((tm, D_MODEL), lambda i: (i, 0)),
        out_shape=jax.ShapeDtypeStruct((rows, D_MODEL), F32),
        compiler_params=pltpu.CompilerParams(
            dimension_semantics=("arbitrary",),
            vmem_limit_bytes=VMEM_LIMIT_BYTES),
    )(t2d, w_out, x2d, p2d, w_proj, gain, w_gate)


def _attn_front_kernel(x_ref, gain_ref, w_ref, cos_ref, sin_ref, qg_ref, kg_ref,
                       q_ref, k_ref, v_ref, zs_ref, hn_ref):
    x = x_ref[...]
    hn_ref[...] = (x * _rms_scale(x) * gain_ref[...]).astype(BF16)

    def norm_rope(slab, head_gain):
        s = slab * _rms_scale(slab) * head_gain
        return s * cos_ref[...] + pltpu.roll(s, HEAD_DIM // 2, axis=1) * sin_ref[...]

    tn = ATTN_FRONT_TN
    heads_per_block = tn // HEAD_DIM
    for n in range(ATTN_IN_WIDTH // tn):
        y = _dot(hn_ref[...], w_ref[:, n * tn:(n + 1) * tn])
        if n < Q_WIDTH // tn:
            for h in range(heads_per_block):
                cols = slice(h * HEAD_DIM, (h + 1) * HEAD_DIM)
                q_ref[:, n * tn + h * HEAD_DIM:n * tn + (h + 1) * HEAD_DIM] = (
                    norm_rope(y[:, cols], qg_ref[...]).astype(BF16))
        elif n == Q_WIDTH // tn:
            for h in range(N_KV_HEADS):
                cols = slice(h * HEAD_DIM, (h + 1) * HEAD_DIM)
                k_ref[:, cols] = norm_rope(y[:, cols], kg_ref[...]).astype(BF16)
            v_ref[...] = y[:, KV_WIDTH:].astype(BF16)
        else:
            z0 = (n - Q_WIDTH // tn - 1) * tn
            zs_ref[:, z0:z0 + tn] = _silu(y)


def _attn_front(x2d, gain, w_in, cos_t, sin_t, q_gain, k_gain, *, seq):
    rows = x2d.shape[0]
    tm = ATTN_FRONT_TM
    assert 2 * KV_WIDTH == ATTN_FRONT_TN
    seq_blocks = seq // tm
    table_spec = pl.BlockSpec((tm, HEAD_DIM), lambda i: (lax.rem(i, seq_blocks), 0))
    row_spec = lambda width: pl.BlockSpec((tm, width), lambda i: (i, 0))
    return pl.pallas_call(
        _attn_front_kernel,
        name="attn_front",
        grid=(rows // tm,),
        in_specs=[
            row_spec(D_MODEL),
            pl.BlockSpec((1, D_MODEL), lambda i: (0, 0)),
            pl.BlockSpec((D_MODEL, ATTN_IN_WIDTH), lambda i: (0, 0), pipeline_mode=pl.Buffered(1)),
            table_spec, table_spec,
            pl.BlockSpec((1, HEAD_DIM), lambda i: (0, 0)),
            pl.BlockSpec((1, HEAD_DIM), lambda i: (0, 0)),
        ],
        out_specs=[row_spec(Q_WIDTH), row_spec(KV_WIDTH), row_spec(KV_WIDTH), row_spec(Q_WIDTH)],
        out_shape=[
            jax.ShapeDtypeStruct((rows, Q_WIDTH), BF16),
            jax.ShapeDtypeStruct((rows, KV_WIDTH), BF16),
            jax.ShapeDtypeStruct((rows, KV_WIDTH), BF16),
            jax.ShapeDtypeStruct((rows, Q_WIDTH), F32),
        ],
        scratch_shapes=[pltpu.VMEM((tm, D_MODEL), BF16)],
        compiler_params=pltpu.CompilerParams(
            dimension_semantics=("arbitrary",),
            vmem_limit_bytes=VMEM_LIMIT_BYTES),
    )(x2d, gain, w_in, cos_t, sin_t, q_gain, k_gain)


def _attn_core_kernel(sink_ref, q_ref, kprev_ref, kmid_ref, knext_ref, vprev_ref, vmid_ref,
                      vnext_ref, zs_ref, o_ref, kbuf_ref, vbuf_ref, bias_ref, *, tm, seq):
    i = pl.program_id(0)
    blk = ATTN_BLOCK
    n_qb = tm // blk
    t0 = lax.rem(i * tm, seq)
    kbuf_ref[0:blk, :] = kprev_ref[...]
    kbuf_ref[blk:blk + tm, :] = kmid_ref[...]
    kbuf_ref[blk + tm:, :] = knext_ref[...]
    for h in range(N_KV_HEADS):
        src = slice(h * HEAD_DIM, (h + 1) * HEAD_DIM)
        dst = slice(2 * h * HEAD_DIM, (2 * h + 1) * HEAD_DIM)
        vbuf_ref[0:blk, dst] = vprev_ref[:, src]
        vbuf_ref[blk:blk + tm, dst] = vmid_ref[:, src]
        vbuf_ref[blk + tm:, dst] = vnext_ref[:, src]
        vbuf_ref[:, (2 * h + 1) * HEAD_DIM:(2 * h + 2) * HEAD_DIM] = jnp.ones(
            (tm + 2 * blk, HEAD_DIM), BF16)

    n_rows = GQA_GROUP * blk
    qi = lax.rem(lax.broadcasted_iota(jnp.int32, (n_rows, blk), 0), blk)
    kj = lax.broadcasted_iota(jnp.int32, (n_rows, blk), 1)
    prev_bias = jnp.where(kj >= qi, 0.0, NEG_BIG).astype(F32)
    next_bias = jnp.where(kj <= qi, 0.0, NEG_BIG).astype(F32)
    bias_ref[0] = jnp.where(t0 == 0, NEG_BIG, prev_bias)
    bias_ref[1] = prev_bias
    bias_ref[2] = jnp.where(t0 + tm == seq, NEG_BIG, next_bias)
    bias_ref[3] = next_bias

    sqrt_d = math.sqrt(HEAD_DIM)
    exp2_scale = LOG2_E / sqrt_d

    def q_block(qb, carry):
        row0 = pl.multiple_of(qb * blk, blk)
        rows = pl.ds(row0, blk)
        b_prev = bias_ref[jnp.where(qb == 0, 0, 1)]
        b_next = bias_ref[jnp.where(qb == n_qb - 1, 2, 3)]
        for kvh in range(N_KV_HEADS):
            heads = [kvh * GQA_GROUP + hg for hg in range(GQA_GROUP)]
            qs = jnp.concatenate(
                [q_ref[rows, h * HEAD_DIM:(h + 1) * HEAD_DIM] for h in heads], axis=0)
            kw = kbuf_ref[pl.ds(row0, 3 * blk), kvh * HEAD_DIM:(kvh + 1) * HEAD_DIM]
            vw = vbuf_ref[pl.ds(row0, 3 * blk), 2 * kvh * HEAD_DIM:(2 * kvh + 2) * HEAD_DIM]
            s = lax.dot_general(qs, kw, (((1,), (1,)), ((), ())), preferred_element_type=F32)
            sb = (s[:, 0:blk] + b_prev, s[:, blk:2 * blk], s[:, 2 * blk:] + b_next)
            row_max = jnp.max(jnp.maximum(jnp.maximum(sb[0], sb[1]), sb[2]), axis=-1, keepdims=True)
            e_rows, m_rows, sinks = [], [], []
            for hg, h in enumerate(heads):
                r = slice(hg * blk, (hg + 1) * blk)
                sink_raw = sink_ref[h] * sqrt_d
                m_h = jnp.maximum(row_max[r], sink_raw)
                e_rows.append(jnp.concatenate(
                    [jnp.exp2((b[r] - m_h) * exp2_scale).astype(BF16) for b in sb], axis=1))
                m_rows.append(m_h)
                sinks.append(sink_raw)
            e = jnp.concatenate(e_rows, axis=0)
            ov = _dot(e, vw)
            for hg, h in enumerate(heads):
                r = slice(hg * blk, (hg + 1) * blk)
                hcols = slice(h * HEAD_DIM, (h + 1) * HEAD_DIM)
                denom = ov[r, HEAD_DIM:] + jnp.exp2((sinks[hg] - m_rows[hg]) * exp2_scale)
                o_ref[rows, hcols] = (ov[r, :HEAD_DIM] * (1.0 / denom) * zs_ref[rows, hcols]
                                      ).astype(BF16)
        return carry

    lax.fori_loop(0, n_qb, q_block, 0)


def _attn_core(sink, q, k, v, zs, *, seq):
    rows = q.shape[0]
    tm = ATTN_CORE_TM
    blk = ATTN_BLOCK
    per = tm // blk
    n_blocks = rows // blk
    prev_map = lambda i: (jnp.maximum(i * per - 1, 0), 0)
    next_map = lambda i: (jnp.minimum((i + 1) * per, n_blocks - 1), 0)
    mid_map = lambda i: (i, 0)
    kernel = functools.partial(_attn_core_kernel, tm=tm, seq=seq)
    return pl.pallas_call(
        kernel,
        name="attn_core",
        grid=(rows // tm,),
        in_specs=[
            pl.BlockSpec(memory_space=pltpu.SMEM),
            pl.BlockSpec((tm, Q_WIDTH), mid_map),
            pl.BlockSpec((blk, KV_WIDTH), prev_map),
            pl.BlockSpec((tm, KV_WIDTH), mid_map),
            pl.BlockSpec((blk, KV_WIDTH), next_map),
            pl.BlockSpec((blk, KV_WIDTH), prev_map),
            pl.BlockSpec((tm, KV_WIDTH), mid_map),
            pl.BlockSpec((blk, KV_WIDTH), next_map),
            pl.BlockSpec((tm, Q_WIDTH), mid_map),
        ],
        out_specs=pl.BlockSpec((tm, Q_WIDTH), mid_map),
        out_shape=jax.ShapeDtypeStruct((rows, Q_WIDTH), BF16),
        scratch_shapes=[
            pltpu.VMEM((tm + 2 * blk, KV_WIDTH), BF16),
            pltpu.VMEM((tm + 2 * blk, 2 * KV_WIDTH), BF16),
            pltpu.VMEM((4, GQA_GROUP * blk, blk), F32),
        ],
        compiler_params=pltpu.CompilerParams(
            dimension_semantics=("arbitrary",),
            vmem_limit_bytes=VMEM_LIMIT_BYTES),
    )(sink, q, k, k, k, v, v, v, zs)


def _rotary_head_perm(a):
    half = ROPE_DIM // 2
    split = HEAD_DIM // 2 - half
    rest = a[..., ROPE_DIM:]
    return jnp.concatenate([a[..., :half], rest[..., :split], a[..., half:ROPE_DIM], rest[..., split:]],
                           axis=-1)


def _rope_tables(seq):
    freq = ROPE_THETA ** (-jnp.arange(0, ROPE_DIM, 2, dtype=F32) / ROPE_DIM)
    pos = jnp.arange(seq, dtype=F32)
    ang = pos[:, None] * freq[None, :]
    cos = jnp.cos(ang)
    sin = jnp.sin(ang)
    rest = HEAD_DIM - ROPE_DIM
    cos_t = jnp.concatenate([cos, cos, jnp.ones((seq, rest), F32)], axis=1)
    sin_t = jnp.concatenate([-sin, sin, jnp.zeros((seq, rest), F32)], axis=1)
    return _rotary_head_perm(cos_t), _rotary_head_perm(sin_t)


def _permute_qk_heads(w_in):
    qk = w_in[:, :Q_WIDTH + KV_WIDTH].reshape(D_MODEL, N_Q_HEADS + N_KV_HEADS, HEAD_DIM)
    qk = _rotary_head_perm(qk).reshape(D_MODEL, Q_WIDTH + KV_WIDTH)
    return jnp.concatenate([qk, w_in[:, Q_WIDTH + KV_WIDTH:]], axis=1)


def _trunk(x, p, w, *, seq):
    batch = x.shape[0]
    rows = batch * seq
    x2d = x.reshape(rows, D_MODEL)
    p2d = p.reshape(p.shape[0] * rows, PLE_DIM)
    p_blocks_per_layer = rows // OUT_TM

    t = _pool_front(x2d, w["norm_g"][0:1], w["a_w_in"], w["a_w_grp"], w["a_scale"], seq=seq)
    x2d = _out_pe(t, w["a_w_out"], x2d, p2d, 0, w["pe_w_proj"][0], w["pe_norm_g"][0:1],
                  w["pe_w_gate"][0])

    cos_t, sin_t = _rope_tables(seq)
    q, k, v, zs = _attn_front(x2d, w["norm_g"][1:2], w["b_w_in"], cos_t, sin_t,
                              w["b_q_norm"], w["b_k_norm"], seq=seq)
    og = _attn_core(w["b_sink"], q, k, v, zs, seq=seq)
    x2d = _out_pe(og, w["b_w_out"], x2d, p2d, p_blocks_per_layer, w["pe_w_proj"][1],
                  w["pe_norm_g"][1:2], w["pe_w_gate"][1])
    return x2d.reshape(batch, seq, D_MODEL)


def kernel(x_prompt, x_sample, p_prompt, p_sample, norm_g, a_w_in, a_w_grp, a_scale, a_w_out,
           b_w_in, b_q_norm, b_k_norm, b_sink, b_w_out, pe_w_proj, pe_norm_g, pe_w_gate):
    w = {
        "norm_g": norm_g,
        "a_w_in": a_w_in[0].astype(BF16),
        "a_w_grp": a_w_grp[0].astype(BF16),
        "a_scale": a_scale,
        "a_w_out": a_w_out[0].astype(BF16),
        "b_w_in": _permute_qk_heads(b_w_in[0]).astype(BF16),
        "b_q_norm": _rotary_head_perm(b_q_norm),
        "b_k_norm": _rotary_head_perm(b_k_norm),
        "b_sink": b_sink[0],
        "b_w_out": b_w_out[0].astype(BF16),
        "pe_w_proj": pe_w_proj.astype(BF16),
        "pe_norm_g": pe_norm_g,
        "pe_w_gate": pe_w_gate.astype(BF16),
    }
    y_prompt = _trunk(x_prompt, p_prompt, w, seq=x_prompt.shape[1])
    y_sample = _trunk(x_sample, p_sample, w, seq=x_sample.shape[1])
    return (y_prompt, y_sample)
```

```python
import functools
import math

import jax
import jax.numpy as jnp
from jax import lax
from jax.experimental import pallas as pl
from jax.experimental.pallas import tpu as pltpu

F32 = jnp.float32
BF16 = jnp.bfloat16

D_MODEL = 2048
PLE_DIM = 256
EPS = 1e-6

POOL_WIDTH = 4096
POOL_WINDOWS = (2, 4, 8, 16)
N_POOL_GROUPS = len(POOL_WINDOWS)
POOL_GROUP_WIDTH = POOL_WIDTH // N_POOL_GROUPS
POOL_HALO = 8

HEAD_DIM = 128
N_Q_HEADS = 16
N_KV_HEADS = 4
GQA_GROUP = N_Q_HEADS // N_KV_HEADS
Q_WIDTH = N_Q_HEADS * HEAD_DIM
KV_WIDTH = N_KV_HEADS * HEAD_DIM
ATTN_IN_WIDTH = 2 * Q_WIDTH + 2 * KV_WIDTH
ATTN_BLOCK = 128
ROPE_DIM = HEAD_DIM // 4
ROPE_THETA = 500000.0
NEG_BIG = -1e30
LOG2_E = math.log2(math.e)

VMEM_LIMIT_BYTES = 56 * 1024 * 1024
LANES = 128
SUBLANES = 8

POOL_TM = 512
ATTN_FRONT_TM = 512
ATTN_FRONT_TN = 1024
ATTN_CORE_TM = 512
OUT_TM_CHOICES = (512, 256)


def _rms_scale(x):
    return lax.rsqrt(jnp.mean(x * x, axis=-1, keepdims=True) + EPS)


def _dot(a, b):
    return jnp.dot(a, b, preferred_element_type=F32)


def _silu(z):
    return z * jax.nn.sigmoid(z)


def _pool_front_kernel(x_ref, xprev_ref, xnext_ref, gain_ref, wu_ref, wz_ref, wgrp_ref,
                       scale_ref, t_ref, hn_ref, a_ref, ext_ref, *, tm, seq):
    i = pl.program_id(0)
    g = pl.program_id(1)
    n_ext = tm + 2 * POOL_HALO

    @pl.when(g == 0)
    def _():
        t0 = lax.rem(i * tm, seq)
        has_prev = t0 > 0
        has_next = t0 + tm < seq
        x = x_ref[...]
        r = _rms_scale(x)
        xp = xprev_ref[...]
        rp = jnp.where(has_prev, _rms_scale(xp), 0.0)
        xn = xnext_ref[...]
        rn = jnp.where(has_next, _rms_scale(xn), 0.0)
        edge = 2 * SUBLANES
        row = lax.broadcasted_iota(jnp.int32, (edge, 1), 0)
        inv_first, inv_last = [], []
        for win in POOL_WINDOWS:
            half = win // 2
            for first_pos, out in ((t0 + row, inv_first), (t0 + tm - edge + row, inv_last)):
                cnt = jnp.minimum(first_pos + half, seq) - jnp.maximum(first_pos - half, 0)
                out.append(1.0 / cnt.astype(F32))

        pad = POOL_HALO
        zeros = jnp.zeros((pad, LANES), F32)
        for buf in range(3):
            ext_ref[buf, 0:pad, :] = zeros
            ext_ref[buf, pad + n_ext:, :] = zeros
        centre = slice(pad + POOL_HALO, pad + POOL_HALO + tm)

        def col_chunk(c, carry):
            cols = pl.ds(pl.multiple_of(c * LANES, LANES), LANES)
            gain = gain_ref[:, cols]
            hc = x_ref[:, cols] * r * gain
            hn_ref[:, cols] = hc.astype(BF16)
            ext_ref[0, pad:pad + POOL_HALO, :] = xprev_ref[:, cols] * rp * gain
            ext_ref[0, centre, :] = hc
            ext_ref[0, pad + POOL_HALO + tm:pad + n_ext, :] = xnext_ref[:, cols] * rn * gain
            src = 0
            for gi, win in enumerate(POOL_WINDOWS):
                lo, hi = (1, 0) if gi == 0 else (win // 4, win // 4)
                level = (ext_ref[src, pad - lo:pad - lo + n_ext, :]
                         + ext_ref[src, pad + hi:pad + hi + n_ext, :])
                if gi + 1 < N_POOL_GROUPS:
                    src = 1 + gi % 2
                    ext_ref[src, pad:pad + n_ext, :] = level
                wsum = level[POOL_HALO:POOL_HALO + tm]
                h = ext_ref[0, centre, :]
                a_ref[gi, :, cols] = (wsum * (1.0 / win) - h).astype(BF16)
                a_ref[gi, 0:edge, cols] = (wsum[:edge] * inv_first[gi] - h[:edge]).astype(BF16)
                a_ref[gi, tm - edge:tm, cols] = (wsum[tm - edge:] * inv_last[gi] - h[tm - edge:]
                                                 ).astype(BF16)
            return carry

        lax.fori_loop(0, D_MODEL // LANES, col_chunk, 0)

    m = _dot(a_ref[g], wu_ref[...])
    z = _dot(hn_ref[...], wz_ref[...])
    mm = _dot(m.astype(BF16), wgrp_ref[0])
    t_ref[...] = ((mm * scale_ref[...]) * _silu(z)).astype(BF16)


def _pool_front(x2d, gain, w_in, w_grp, scale, *, seq):
    rows = x2d.shape[0]
    tm = POOL_TM
    c = POOL_GROUP_WIDTH
    n_row_blocks8 = rows // SUBLANES
    kernel = functools.partial(_pool_front_kernel, tm=tm, seq=seq)
    return pl.pallas_call(
        kernel,
        name="pool_front",
        grid=(rows // tm, N_POOL_GROUPS),
        in_specs=[
            pl.BlockSpec((tm, D_MODEL), lambda i, g: (i, 0)),
            pl.BlockSpec((POOL_HALO, D_MODEL),
                         lambda i, g: (jnp.maximum(i * (tm // SUBLANES) - 1, 0), 0)),
            pl.BlockSpec((POOL_HALO, D_MODEL),
                         lambda i, g: (jnp.minimum((i + 1) * (tm // SUBLANES), n_row_blocks8 - 1), 0)),
            pl.BlockSpec((1, D_MODEL), lambda i, g: (0, 0)),
            pl.BlockSpec((D_MODEL, c), lambda i, g: (0, g)),
            pl.BlockSpec((D_MODEL, c), lambda i, g: (0, N_POOL_GROUPS + g)),
            pl.BlockSpec((1, c, c), lambda i, g: (g, 0, 0)),
            pl.BlockSpec((1, c), lambda i, g: (0, g)),
        ],
        out_specs=pl.BlockSpec((tm, c), lambda i, g: (i, g)),
        out_shape=jax.ShapeDtypeStruct((rows, POOL_WIDTH), BF16),
        scratch_shapes=[
            pltpu.VMEM((tm, D_MODEL), BF16),
            pltpu.VMEM((N_POOL_GROUPS, tm, D_MODEL), BF16),
            pltpu.VMEM((3, tm + 4 * POOL_HALO, LANES), F32),
        ],
        compiler_params=pltpu.CompilerParams(
            dimension_semantics=("arbitrary", "arbitrary"),
            vmem_limit_bytes=VMEM_LIMIT_BYTES),
    )(x2d, x2d, x2d, gain, w_in, w_in, w_grp, scale)


def _out_pe_kernel(t_ref, wout_ref, x_ref, p_ref, wp_ref, gain_ref, wgate_ref, out_ref):
    h = x_ref[...] + _dot(t_ref[...], wout_ref[...])
    hn = (h * _rms_scale(h) * gain_ref[...]).astype(BF16)
    gate = jax.nn.sigmoid(_dot(hn, wgate_ref[...]))
    e = _dot(p_ref[...].astype(BF16), wp_ref[...])
    out_ref[...] = h + e * gate


def _out_pe_tile(k):
    weights = 2 * (k * D_MODEL + PLE_DIM * D_MODEL + D_MODEL * D_MODEL)
    for tm in OUT_TM_CHOICES:
        row_blocks = 2 * tm * (2 * k + 4 * D_MODEL + 4 * PLE_DIM + 4 * D_MODEL)
        temporaries = 3 * tm * D_MODEL * 4
        if weights + row_blocks + temporaries <= VMEM_LIMIT_BYTES:
            return tm
    raise ValueError(f"no out_pe row tile fits VMEM for k={k}")


def _out_pe(t2d, w_out, x2d, p2d, layer, w_proj, gain, w_gate):
    rows = x2d.shape[0]
    k = t2d.shape[1]
    tm = _out_pe_tile(k)
    p_block_offset = layer * (rows // tm)
    resident = pl.Buffered(1)
    return pl.pallas_call(
        _out_pe_kernel,
        name="out_pe",
        grid=(rows // tm,),
        in_specs=[
            pl.BlockSpec((tm, k), lambda i: (i, 0)),
            pl.BlockSpec((k, D_MODEL), lambda i: (0, 0), pipeline_mode=resident),
            pl.BlockSpec((tm, D_MODEL), lambda i: (i, 0)),
            pl.BlockSpec((tm, PLE_DIM), lambda i: (i + p_block_offset, 0)),
            pl.BlockSpec((PLE_DIM, D_MODEL), lambda i: (0, 0), pipeline_mode=resident),
            pl.BlockSpec((1, D_MODEL), lambda i: (0, 0)),
            pl.BlockSpec((D_MODEL, D_MODEL), lambda i: (0, 0), pipeline_mode=resident),
        ],
        out_specs=pl.BlockSpec((tm, D_MODEL), lambda i: (i, 0)),
        out_shape=jax.ShapeDtypeStruct((rows, D_MODEL), F32),
        compiler_params=pltpu.CompilerParams(
            dimension_semantics=("arbitrary",),
            vmem_limit_bytes=VMEM_LIMIT_BYTES),
    )(t2d, w_out, x2d, p2d, w_proj, gain, w_gate)


def _attn_front_kernel(x_ref, gain_ref, w_ref, cos_ref, sin_ref, qg_ref, kg_ref,
                       q_ref, k_ref, v_ref, zs_ref, hn_ref):
    x = x_ref[...]
    hn_ref[...] = (x * _rms_scale(x) * gain_ref[...]).astype(BF16)

    def norm_rope(slab, head_gain):
        s = slab * _rms_scale(slab) * head_gain
        return s * cos_ref[...] + pltpu.roll(s, HEAD_DIM // 2, axis=1) * sin_ref[...]

    tn = ATTN_FRONT_TN
    heads_per_block = tn // HEAD_DIM
    for n in range(ATTN_IN_WIDTH // tn):
        y = _dot(hn_ref[...], w_ref[:, n * tn:(n + 1) * tn])
        if n < Q_WIDTH // tn:
            for h in range(heads_per_block):
                cols = slice(h * HEAD_DIM, (h + 1) * HEAD_DIM)
                q_ref[:, n * tn + h * HEAD_DIM:n * tn + (h + 1) * HEAD_DIM] = (
                    norm_rope(y[:, cols], qg_ref[...]).astype(BF16))
        elif n == Q_WIDTH // tn:
            for h in range(N_KV_HEADS):
                cols = slice(h * HEAD_DIM, (h + 1) * HEAD_DIM)
                k_ref[:, cols] = norm_rope(y[:, cols], kg_ref[...]).astype(BF16)
            v_ref[...] = y[:, KV_WIDTH:].astype(BF16)
        else:
            z0 = (n - Q_WIDTH // tn - 1) * tn
            zs_ref[:, z0:z0 + tn] = _silu(y)


def _attn_front(x2d, gain, w_in, cos_t, sin_t, q_gain, k_gain, *, seq):
    rows = x2d.shape[0]
    tm = ATTN_FRONT_TM
    assert 2 * KV_WIDTH == ATTN_FRONT_TN
    seq_blocks = seq // tm
    table_spec = pl.BlockSpec((tm, HEAD_DIM), lambda i: (lax.rem(i, seq_blocks), 0))
    row_spec = lambda width: pl.BlockSpec((tm, width), lambda i: (i, 0))
    return pl.pallas_call(
        _attn_front_kernel,
        name="attn_front",
        grid=(rows // tm,),
        in_specs=[
            row_spec(D_MODEL),
            pl.BlockSpec((1, D_MODEL), lambda i: (0, 0)),
            pl.BlockSpec((D_MODEL, ATTN_IN_WIDTH), lambda i: (0, 0), pipeline_mode=pl.Buffered(1)),
            table_spec, table_spec,
            pl.BlockSpec((1, HEAD_DIM), lambda i: (0, 0)),
            pl.BlockSpec((1, HEAD_DIM), lambda i: (0, 0)),
        ],
        out_specs=[row_spec(Q_WIDTH), row_spec(KV_WIDTH), row_spec(KV_WIDTH), row_spec(Q_WIDTH)],
        out_shape=[
            jax.ShapeDtypeStruct((rows, Q_WIDTH), BF16),
            jax.ShapeDtypeStruct((rows, KV_WIDTH), BF16),
            jax.ShapeDtypeStruct((rows, KV_WIDTH), BF16),
            jax.ShapeDtypeStruct((rows, Q_WIDTH), F32),
        ],
        scratch_shapes=[pltpu.VMEM((tm, D_MODEL), BF16)],
        compiler_params=pltpu.CompilerParams(
            dimension_semantics=("arbitrary",),
            vmem_limit_bytes=VMEM_LIMIT_BYTES),
    )(x2d, gain, w_in, cos_t, sin_t, q_gain, k_gain)


def _attn_core_kernel(sink_ref, q_ref, kprev_ref, kmid_ref, knext_ref, vprev_ref, vmid_ref,
                      vnext_ref, zs_ref, o_ref, kbuf_ref, vbuf_ref, bias_ref, *, tm, seq):
    i = pl.program_id(0)
    blk = ATTN_BLOCK
    n_qb = tm // blk
    t0 = lax.rem(i * tm, seq)
    kbuf_ref[0:blk, :] = kprev_ref[...]
    kbuf_ref[blk:blk + tm, :] = kmid_ref[...]
    kbuf_ref[blk + tm:, :] = knext_ref[...]
    for h in range(N_KV_HEADS):
        src = slice(h * HEAD_DIM, (h + 1) * HEAD_DIM)
        dst = slice(2 * h * HEAD_DIM, (2 * h + 1) * HEAD_DIM)
        vbuf_ref[0:blk, dst] = vprev_ref[:, src]
        vbuf_ref[blk:blk + tm, dst] = vmid_ref[:, src]
        vbuf_ref[blk + tm:, dst] = vnext_ref[:, src]
        vbuf_ref[:, (2 * h + 1) * HEAD_DIM:(2 * h + 2) * HEAD_DIM] = jnp.ones(
            (tm + 2 * blk, HEAD_DIM), BF16)

    n_rows = GQA_GROUP * blk
    qi = lax.rem(lax.broadcasted_iota(jnp.int32, (n_rows, blk), 0), blk)
    kj = lax.broadcasted_iota(jnp.int32, (n_rows, blk), 1)
    prev_bias = jnp.where(kj >= qi, 0.0, NEG_BIG).astype(F32)
    next_bias = jnp.where(kj <= qi, 0.0, NEG_BIG).astype(F32)
    bias_ref[0] = jnp.where(t0 == 0, NEG_BIG, prev_bias)
    bias_ref[1] = prev_bias
    bias_ref[2] = jnp.where(t0 + tm == seq, NEG_BIG, next_bias)
    bias_ref[3] = next_bias

    sqrt_d = math.sqrt(HEAD_DIM)
    exp2_scale = LOG2_E / sqrt_d

    for qb in range(n_qb):
        row0 = qb * blk
        rows = pl.ds(row0, blk)
        b_prev = bias_ref[0 if qb == 0 else 1]
        b_next = bias_ref[2 if qb == n_qb - 1 else 3]
        for kvh in range(N_KV_HEADS):
            heads = [kvh * GQA_GROUP + hg for hg in range(GQA_GROUP)]
            qs = jnp.concatenate(
                [q_ref[rows, h * HEAD_DIM:(h + 1) * HEAD_DIM] for h in heads], axis=0)
            kw = kbuf_ref[pl.ds(row0, 3 * blk), kvh * HEAD_DIM:(kvh + 1) * HEAD_DIM]
            vw = vbuf_ref[pl.ds(row0, 3 * blk), 2 * kvh * HEAD_DIM:(2 * kvh + 2) * HEAD_DIM]
            s = lax.dot_general(qs, kw, (((1,), (1,)), ((), ())), preferred_element_type=F32)
            sb = (s[:, 0:blk] + b_prev, s[:, blk:2 * blk], s[:, 2 * blk:] + b_next)
            row_max = jnp.max(jnp.maximum(jnp.maximum(sb[0], sb[1]), sb[2]), axis=-1, keepdims=True)
            e_rows, m_rows, sinks = [], [], []
            for hg, h in enumerate(heads):
                r = slice(hg * blk, (hg + 1) * blk)
                sink_raw = sink_ref[h] * sqrt_d
                m_h = jnp.maximum(row_max[r], sink_raw)
                e_rows.append(jnp.concatenate(
                    [jnp.exp2((b[r] - m_h) * exp2_scale).astype(BF16) for b in sb], axis=1))
                m_rows.append(m_h)
                sinks.append(sink_raw)
            e = jnp.concatenate(e_rows, axis=0)
            ov = _dot(e, vw)
            for hg, h in enumerate(heads):
                r = slice(hg * blk, (hg + 1) * blk)
                hcols = slice(h * HEAD_DIM, (h + 1) * HEAD_DIM)
                denom = ov[r, HEAD_DIM:] + jnp.exp2((sinks[hg] - m_rows[hg]) * exp2_scale)
                o_ref[rows, hcols] = (ov[r, :HEAD_DIM] * (1.0 / denom) * zs_ref[rows, hcols]
                                      ).astype(BF16)


def _attn_core(sink, q, k, v, zs, *, seq):
    rows = q.shape[0]
    tm = ATTN_CORE_TM
    blk = ATTN_BLOCK
    per = tm // blk
    n_blocks = rows // blk
    prev_map = lambda i: (jnp.maximum(i * per - 1, 0), 0)
    next_map = lambda i: (jnp.minimum((i + 1) * per, n_blocks - 1), 0)
    mid_map = lambda i: (i, 0)
    kernel = functools.partial(_attn_core_kernel, tm=tm, seq=seq)
    return pl.pallas_call(
        kernel,
        name="attn_core",
        grid=(rows // tm,),
        in_specs=[
            pl.BlockSpec(memory_space=pltpu.SMEM),
            pl.BlockSpec((tm, Q_WIDTH), mid_map),
            pl.BlockSpec((blk, KV_WIDTH), prev_map),
            pl.BlockSpec((tm, KV_WIDTH), mid_map),
            pl.BlockSpec((blk, KV_WIDTH), next_map),
            pl.BlockSpec((blk, KV_WIDTH), prev_map),
            pl.BlockSpec((tm, KV_WIDTH), mid_map),
            pl.BlockSpec((blk, KV_WIDTH), next_map),
            pl.BlockSpec((tm, Q_WIDTH), mid_map),
        ],
        out_specs=pl.BlockSpec((tm, Q_WIDTH), mid_map),
        out_shape=jax.ShapeDtypeStruct((rows, Q_WIDTH), BF16),
        scratch_shapes=[
            pltpu.VMEM((tm + 2 * blk, KV_WIDTH), BF16),
            pltpu.VMEM((tm + 2 * blk, 2 * KV_WIDTH), BF16),
            pltpu.VMEM((4, GQA_GROUP * blk, blk), F32),
        ],
        compiler_params=pltpu.CompilerParams(
            dimension_semantics=("arbitrary",),
            vmem_limit_bytes=VMEM_LIMIT_BYTES),
    )(sink, q, k, k, k, v, v, v, zs)


def _rotary_head_perm(a):
    half = ROPE_DIM // 2
    split = HEAD_DIM // 2 - half
    rest = a[..., ROPE_DIM:]
    return jnp.concatenate([a[..., :half], rest[..., :split], a[..., half:ROPE_DIM], rest[..., split:]],
                           axis=-1)


def _rope_tables(seq):
    freq = ROPE_THETA ** (-jnp.arange(0, ROPE_DIM, 2, dtype=F32) / ROPE_DIM)
    pos = jnp.arange(seq, dtype=F32)
    ang = pos[:, None] * freq[None, :]
    cos = jnp.cos(ang)
    sin = jnp.sin(ang)
    rest = HEAD_DIM - ROPE_DIM
    cos_t = jnp.concatenate([cos, cos, jnp.ones((seq, rest), F32)], axis=1)
    sin_t = jnp.concatenate([-sin, sin, jnp.zeros((seq, rest), F32)], axis=1)
    return _rotary_head_perm(cos_t), _rotary_head_perm(sin_t)


def _permute_qk_heads(w_in):
    qk = w_in[:, :Q_WIDTH + KV_WIDTH].reshape(D_MODEL, N_Q_HEADS + N_KV_HEADS, HEAD_DIM)
    qk = _rotary_head_perm(qk).reshape(D_MODEL, Q_WIDTH + KV_WIDTH)
    return jnp.concatenate([qk, w_in[:, Q_WIDTH + KV_WIDTH:]], axis=1)


def _trunk(x, p, w, *, seq):
    batch = x.shape[0]
    rows = batch * seq
    x2d = x.reshape(rows, D_MODEL)
    p2d = p.reshape(p.shape[0] * rows, PLE_DIM)

    t = _pool_front(x2d, w["norm_g"][0:1], w["a_w_in"], w["a_w_grp"], w["a_scale"], seq=seq)
    x2d = _out_pe(t, w["a_w_out"], x2d, p2d, 0, w["pe_w_proj"][0], w["pe_norm_g"][0:1],
                  w["pe_w_gate"][0])

    q, k, v, zs = _attn_front(x2d, w["norm_g"][1:2], w["b_w_in"], w["rope_cos"], w["rope_sin"],
                              w["b_q_norm"], w["b_k_norm"], seq=seq)
    og = _attn_core(w["b_sink"], q, k, v, zs, seq=seq)
    x2d = _out_pe(og, w["b_w_out"], x2d, p2d, 1, w["pe_w_proj"][1],
                  w["pe_norm_g"][1:2], w["pe_w_gate"][1])
    return x2d.reshape(batch, seq, D_MODEL)


def kernel(x_prompt, x_sample, p_prompt, p_sample, norm_g, a_w_in, a_w_grp, a_scale, a_w_out,
           b_w_in, b_q_norm, b_k_norm, b_sink, b_w_out, pe_w_proj, pe_norm_g, pe_w_gate):
    w = {
        "norm_g": norm_g,
        "a_w_in": a_w_in[0].astype(BF16),
        "a_w_grp": a_w_grp[0].astype(BF16),
        "a_scale": a_scale,
        "a_w_out": a_w_out[0].astype(BF16),
        "b_w_in": _permute_qk_heads(b_w_in[0].astype(BF16)),
        "b_q_norm": _rotary_head_perm(b_q_norm),
        "b_k_norm": _rotary_head_perm(b_k_norm),
        "b_sink": b_sink[0],
        "b_w_out": b_w_out[0].astype(BF16),
        "pe_w_proj": pe_w_proj.astype(BF16),
        "pe_norm_g": pe_norm_g,
        "pe_w_gate": pe_w_gate.astype(BF16),
    }
    w["rope_cos"], w["rope_sin"] = _rope_tables(max(x_prompt.shape[1], x_sample.shape[1]))
    y_prompt = _trunk(x_prompt, p_prompt, w, seq=x_prompt.shape[1])
    y_sample = _trunk(x_sample, p_sample, w, seq=x_sample.shape[1])
    return (y_prompt, y_sample)
```

```python
import functools
import math

import jax
import jax.numpy as jnp
from jax import lax
from jax.experimental import pallas as pl
from jax.experimental.pallas import tpu as pltpu

F32 = jnp.float32
BF16 = jnp.bfloat16

D_MODEL = 2048
PLE_DIM = 256
EPS = 1e-6

POOL_WIDTH = 4096
POOL_WINDOWS = (2, 4, 8, 16)
N_POOL_GROUPS = len(POOL_WINDOWS)
POOL_GROUP_WIDTH = POOL_WIDTH // N_POOL_GROUPS
POOL_HALO = 8

HEAD_DIM = 128
N_Q_HEADS = 16
N_KV_HEADS = 4
GQA_GROUP = N_Q_HEADS // N_KV_HEADS
Q_WIDTH = N_Q_HEADS * HEAD_DIM
KV_WIDTH = N_KV_HEADS * HEAD_DIM
ATTN_IN_WIDTH = 2 * Q_WIDTH + 2 * KV_WIDTH
ATTN_BLOCK = 128
ROPE_DIM = HEAD_DIM // 4
ROPE_THETA = 500000.0
NEG_BIG = -1e30
LOG2_E = math.log2(math.e)

VMEM_LIMIT_BYTES = 56 * 1024 * 1024
LANES = 128
SUBLANES = 8

POOL_TM = 512
ATTN_FRONT_TM = 512
ATTN_FRONT_TN = 1024
ATTN_CORE_TM = 512
OUT_TM_CHOICES = (512, 256)


def _rms_scale(x):
    return lax.rsqrt(jnp.mean(x * x, axis=-1, keepdims=True) + EPS)


def _dot(a, b):
    return jnp.dot(a, b, preferred_element_type=F32)


def _silu(z):
    return z * jax.nn.sigmoid(z)


def _pool_front_kernel(x_ref, xprev_ref, xnext_ref, gain_ref, wu_ref, wz_ref, wgrp_ref,
                       scale_ref, t_ref, hn_ref, a_ref, ext_ref, *, tm, seq):
    i = pl.program_id(0)
    g = pl.program_id(1)
    n_ext = tm + 2 * POOL_HALO

    @pl.when(g == 0)
    def _():
        t0 = lax.rem(i * tm, seq)
        has_prev = t0 > 0
        has_next = t0 + tm < seq
        x = x_ref[...]
        r = _rms_scale(x)
        xp = xprev_ref[...]
        rp = jnp.where(has_prev, _rms_scale(xp), 0.0)
        xn = xnext_ref[...]
        rn = jnp.where(has_next, _rms_scale(xn), 0.0)
        edge = 2 * SUBLANES
        row = lax.broadcasted_iota(jnp.int32, (edge, 1), 0)
        inv_first, inv_last = [], []
        for win in POOL_WINDOWS:
            half = win // 2
            for first_pos, out in ((t0 + row, inv_first), (t0 + tm - edge + row, inv_last)):
                cnt = jnp.minimum(first_pos + half, seq) - jnp.maximum(first_pos - half, 0)
                out.append(1.0 / cnt.astype(F32))

        pad = POOL_HALO
        zeros = jnp.zeros((pad, LANES), F32)
        for buf in range(3):
            ext_ref[buf, 0:pad, :] = zeros
            ext_ref[buf, pad + n_ext:, :] = zeros
        centre = slice(pad + POOL_HALO, pad + POOL_HALO + tm)

        def col_chunk(c, carry):
            cols = pl.ds(pl.multiple_of(c * LANES, LANES), LANES)
            gain = gain_ref[:, cols]
            hc = x_ref[:, cols] * r * gain
            hn_ref[:, cols] = hc.astype(BF16)
            ext_ref[0, pad:pad + POOL_HALO, :] = xprev_ref[:, cols] * rp * gain
            ext_ref[0, centre, :] = hc
            ext_ref[0, pad + POOL_HALO + tm:pad + n_ext, :] = xnext_ref[:, cols] * rn * gain
            src = 0
            for gi, win in enumerate(POOL_WINDOWS):
                lo, hi = (1, 0) if gi == 0 else (win // 4, win // 4)
                level = (ext_ref[src, pad - lo:pad - lo + n_ext, :]
                         + ext_ref[src, pad + hi:pad + hi + n_ext, :])
                if gi + 1 < N_POOL_GROUPS:
                    src = 1 + gi % 2
                    ext_ref[src, pad:pad + n_ext, :] = level
                wsum = level[POOL_HALO:POOL_HALO + tm]
                h = ext_ref[0, centre, :]
                a_ref[gi, :, cols] = (wsum * (1.0 / win) - h).astype(BF16)
                a_ref[gi, 0:edge, cols] = (wsum[:edge] * inv_first[gi] - h[:edge]).astype(BF16)
                a_ref[gi, tm - edge:tm, cols] = (wsum[tm - edge:] * inv_last[gi] - h[tm - edge:]
                                                 ).astype(BF16)
            return carry

        lax.fori_loop(0, D_MODEL // LANES, col_chunk, 0)

    m = _dot(a_ref[g], wu_ref[...])
    z = _dot(hn_ref[...], wz_ref[...])
    mm = _dot(m.astype(BF16), wgrp_ref[0])
    t_ref[...] = ((mm * scale_ref[...]) * _silu(z)).astype(BF16)


def _pool_front(x2d, gain, w_in, w_grp, scale, *, seq):
    rows = x2d.shape[0]
    tm = POOL_TM
    c = POOL_GROUP_WIDTH
    n_row_blocks8 = rows // SUBLANES
    kernel = functools.partial(_pool_front_kernel, tm=tm, seq=seq)
    return pl.pallas_call(
        kernel,
        name="pool_front",
        grid=(rows // tm, N_POOL_GROUPS),
        in_specs=[
            pl.BlockSpec((tm, D_MODEL), lambda i, g: (i, 0)),
            pl.BlockSpec((POOL_HALO, D_MODEL),
                         lambda i, g: (jnp.maximum(i * (tm // SUBLANES) - 1, 0), 0)),
            pl.BlockSpec((POOL_HALO, D_MODEL),
                         lambda i, g: (jnp.minimum((i + 1) * (tm // SUBLANES), n_row_blocks8 - 1), 0)),
            pl.BlockSpec((1, D_MODEL), lambda i, g: (0, 0)),
            pl.BlockSpec((D_MODEL, c), lambda i, g: (0, g)),
            pl.BlockSpec((D_MODEL, c), lambda i, g: (0, N_POOL_GROUPS + g)),
            pl.BlockSpec((1, c, c), lambda i, g: (g, 0, 0)),
            pl.BlockSpec((1, c), lambda i, g: (0, g)),
        ],
        out_specs=pl.BlockSpec((tm, c), lambda i, g: (i, g)),
        out_shape=jax.ShapeDtypeStruct((rows, POOL_WIDTH), BF16),
        scratch_shapes=[
            pltpu.VMEM((tm, D_MODEL), BF16),
            pltpu.VMEM((N_POOL_GROUPS, tm, D_MODEL), BF16),
            pltpu.VMEM((3, tm + 4 * POOL_HALO, LANES), F32),
        ],
        compiler_params=pltpu.CompilerParams(
            dimension_semantics=("arbitrary", "arbitrary"),
            vmem_limit_bytes=VMEM_LIMIT_BYTES),
    )(x2d, x2d, x2d, gain, w_in, w_in, w_grp, scale)


def _out_pe_kernel(t_ref, wout_ref, x_ref, p_ref, wp_ref, gain_ref, wgate_ref, out_ref):
    h = x_ref[...] + _dot(t_ref[...], wout_ref[...])
    hn = (h * _rms_scale(h) * gain_ref[...]).astype(BF16)
    gate = jax.nn.sigmoid(_dot(hn, wgate_ref[...]))
    e = _dot(p_ref[...].astype(BF16), wp_ref[...])
    out_ref[...] = h + e * gate


def _out_pe_tile(k):
    weights = 2 * (k * D_MODEL + PLE_DIM * D_MODEL + D_MODEL * D_MODEL)
    for tm in OUT_TM_CHOICES:
        row_blocks = 2 * tm * (2 * k + 4 * D_MODEL + 4 * PLE_DIM + 4 * D_MODEL)
        temporaries = 3 * tm * D_MODEL * 4
        if weights + row_blocks + temporaries <= VMEM_LIMIT_BYTES:
            return tm
    raise ValueError(f"no out_pe row tile fits VMEM for k={k}")


def _out_pe(t2d, w_out, x2d, p2d, layer, w_proj, gain, w_gate):
    rows = x2d.shape[0]
    k = t2d.shape[1]
    tm = _out_pe_tile(k)
    p_block_offset = layer * (rows // tm)
    resident = pl.Buffered(1)
    return pl.pallas_call(
        _out_pe_kernel,
        name="out_pe",
        grid=(rows // tm,),
        in_specs=[
            pl.BlockSpec((tm, k), lambda i: (i, 0)),
            pl.BlockSpec((k, D_MODEL), lambda i: (0, 0), pipeline_mode=resident),
            pl.BlockSpec((tm, D_MODEL), lambda i: (i, 0)),
            pl.BlockSpec((tm, PLE_DIM), lambda i: (i + p_block_offset, 0)),
            pl.BlockSpec((PLE_DIM, D_MODEL), lambda i: (0, 0), pipeline_mode=resident),
            pl.BlockSpec((1, D_MODEL), lambda i: (0, 0)),
            pl.BlockSpec((D_MODEL, D_MODEL), lambda i: (0, 0), pipeline_mode=resident),
        ],
        out_specs=pl.BlockSpec((tm, D_MODEL), lambda i: (i, 0)),
        out_shape=jax.ShapeDtypeStruct((rows, D_MODEL), F32),
        compiler_params=pltpu.CompilerParams(
            dimension_semantics=("arbitrary",),
            vmem_limit_bytes=VMEM_LIMIT_BYTES),
    )(t2d, w_out, x2d, p2d, w_proj, gain, w_gate)


def _attn_front_kernel(x_ref, gain_ref, w_ref, cos_ref, sin_ref, qg_ref, kg_ref,
                       q_ref, k_ref, v_ref, zs_ref, hn_ref):
    x = x_ref[...]
    hn_ref[...] = (x * _rms_scale(x) * gain_ref[...]).astype(BF16)
    cos = cos_ref[...].T
    sin = sin_ref[...].T
    half = ROPE_DIM // 2
    first_half = lax.broadcasted_iota(jnp.int32, cos.shape, 1) < half

    def norm_rope(slab, head_gain):
        s = slab * _rms_scale(slab) * head_gain
        partner = jnp.where(first_half, pltpu.roll(s, HEAD_DIM - half, axis=1),
                            pltpu.roll(s, half, axis=1))
        return s * cos + partner * sin

    tn = ATTN_FRONT_TN
    heads_per_block = tn // HEAD_DIM
    for n in range(ATTN_IN_WIDTH // tn):
        y = _dot(hn_ref[...], w_ref[:, n * tn:(n + 1) * tn])
        if n < Q_WIDTH // tn:
            for h in range(heads_per_block):
                cols = slice(h * HEAD_DIM, (h + 1) * HEAD_DIM)
                q_ref[:, n * tn + h * HEAD_DIM:n * tn + (h + 1) * HEAD_DIM] = (
                    norm_rope(y[:, cols], qg_ref[...]).astype(BF16))
        elif n == Q_WIDTH // tn:
            for h in range(N_KV_HEADS):
                cols = slice(h * HEAD_DIM, (h + 1) * HEAD_DIM)
                k_ref[:, cols] = norm_rope(y[:, cols], kg_ref[...]).astype(BF16)
            v_ref[...] = y[:, KV_WIDTH:].astype(BF16)
        else:
            z0 = (n - Q_WIDTH // tn - 1) * tn
            zs_ref[:, z0:z0 + tn] = _silu(y)


def _attn_front(x2d, gain, w_in, cos_t, sin_t, q_gain, k_gain, *, seq):
    rows = x2d.shape[0]
    tm = ATTN_FRONT_TM
    assert 2 * KV_WIDTH == ATTN_FRONT_TN
    seq_blocks = seq // tm
    table_spec = pl.BlockSpec((HEAD_DIM, tm), lambda i: (0, lax.rem(i, seq_blocks)))
    row_spec = lambda width: pl.BlockSpec((tm, width), lambda i: (i, 0))
    return pl.pallas_call(
        _attn_front_kernel,
        name="attn_front",
        grid=(rows // tm,),
        in_specs=[
            row_spec(D_MODEL),
            pl.BlockSpec((1, D_MODEL), lambda i: (0, 0)),
            pl.BlockSpec((D_MODEL, ATTN_IN_WIDTH), lambda i: (0, 0), pipeline_mode=pl.Buffered(1)),
            table_spec, table_spec,
            pl.BlockSpec((1, HEAD_DIM), lambda i: (0, 0)),
            pl.BlockSpec((1, HEAD_DIM), lambda i: (0, 0)),
        ],
        out_specs=[row_spec(Q_WIDTH), row_spec(KV_WIDTH), row_spec(KV_WIDTH), row_spec(Q_WIDTH)],
        out_shape=[
            jax.ShapeDtypeStruct((rows, Q_WIDTH), BF16),
            jax.ShapeDtypeStruct((rows, KV_WIDTH), BF16),
            jax.ShapeDtypeStruct((rows, KV_WIDTH), BF16),
            jax.ShapeDtypeStruct((rows, Q_WIDTH), F32),
        ],
        scratch_shapes=[pltpu.VMEM((tm, D_MODEL), BF16)],
        compiler_params=pltpu.CompilerParams(
            dimension_semantics=("arbitrary",),
            vmem_limit_bytes=VMEM_LIMIT_BYTES),
    )(x2d, gain, w_in, cos_t, sin_t, q_gain, k_gain)


def _attn_core_kernel(sink_ref, q_ref, kprev_ref, kmid_ref, knext_ref, vprev_ref, vmid_ref,
                      vnext_ref, zs_ref, o_ref, kbuf_ref, vbuf_ref, bias_ref, *, tm, seq):
    i = pl.program_id(0)
    blk = ATTN_BLOCK
    n_qb = tm // blk
    t0 = lax.rem(i * tm, seq)
    kbuf_ref[0:blk, :] = kprev_ref[...]
    kbuf_ref[blk:blk + tm, :] = kmid_ref[...]
    kbuf_ref[blk + tm:, :] = knext_ref[...]
    for h in range(N_KV_HEADS):
        src = slice(h * HEAD_DIM, (h + 1) * HEAD_DIM)
        dst = slice(2 * h * HEAD_DIM, (2 * h + 1) * HEAD_DIM)
        vbuf_ref[0:blk, dst] = vprev_ref[:, src]
        vbuf_ref[blk:blk + tm, dst] = vmid_ref[:, src]
        vbuf_ref[blk + tm:, dst] = vnext_ref[:, src]
        vbuf_ref[:, (2 * h + 1) * HEAD_DIM:(2 * h + 2) * HEAD_DIM] = jnp.ones(
            (tm + 2 * blk, HEAD_DIM), BF16)

    n_rows = GQA_GROUP * blk
    qi = lax.rem(lax.broadcasted_iota(jnp.int32, (n_rows, blk), 0), blk)
    kj = lax.broadcasted_iota(jnp.int32, (n_rows, blk), 1)
    prev_bias = jnp.where(kj >= qi, 0.0, NEG_BIG).astype(F32)
    next_bias = jnp.where(kj <= qi, 0.0, NEG_BIG).astype(F32)
    bias_ref[0] = jnp.where(t0 == 0, NEG_BIG, prev_bias)
    bias_ref[1] = prev_bias
    bias_ref[2] = jnp.where(t0 + tm == seq, NEG_BIG, next_bias)
    bias_ref[3] = next_bias

    sqrt_d = math.sqrt(HEAD_DIM)
    exp2_scale = LOG2_E / sqrt_d

    for qb in range(n_qb):
        row0 = qb * blk
        rows = pl.ds(row0, blk)
        b_prev = bias_ref[0 if qb == 0 else 1]
        b_next = bias_ref[2 if qb == n_qb - 1 else 3]
        for kvh in range(N_KV_HEADS):
            heads = [kvh * GQA_GROUP + hg for hg in range(GQA_GROUP)]
            qs = jnp.concatenate(
                [q_ref[rows, h * HEAD_DIM:(h + 1) * HEAD_DIM] for h in heads], axis=0)
            kw = kbuf_ref[pl.ds(row0, 3 * blk), kvh * HEAD_DIM:(kvh + 1) * HEAD_DIM]
            vw = vbuf_ref[pl.ds(row0, 3 * blk), 2 * kvh * HEAD_DIM:(2 * kvh + 2) * HEAD_DIM]
            s = lax.dot_general(qs, kw, (((1,), (1,)), ((), ())), preferred_element_type=F32)
            sb = (s[:, 0:blk] + b_prev, s[:, blk:2 * blk], s[:, 2 * blk:] + b_next)
            row_max = jnp.max(jnp.maximum(jnp.maximum(sb[0], sb[1]), sb[2]), axis=-1, keepdims=True)
            e_rows, m_rows, sinks = [], [], []
            for hg, h in enumerate(heads):
                r = slice(hg * blk, (hg + 1) * blk)
                sink_raw = sink_ref[h] * sqrt_d
                m_h = jnp.maximum(row_max[r], sink_raw)
                e_rows.append(jnp.concatenate(
                    [jnp.exp2((b[r] - m_h) * exp2_scale).astype(BF16) for b in sb], axis=1))
                m_rows.append(m_h)
                sinks.append(sink_raw)
            e = jnp.concatenate(e_rows, axis=0)
            ov = _dot(e, vw)
            for hg, h in enumerate(heads):
                r = slice(hg * blk, (hg + 1) * blk)
                hcols = slice(h * HEAD_DIM, (h + 1) * HEAD_DIM)
                denom = ov[r, HEAD_DIM:] + jnp.exp2((sinks[hg] - m_rows[hg]) * exp2_scale)
                o_ref[rows, hcols] = (ov[r, :HEAD_DIM] * (1.0 / denom) * zs_ref[rows, hcols]
                                      ).astype(BF16)


def _attn_core(sink, q, k, v, zs, *, seq):
    rows = q.shape[0]
    tm = ATTN_CORE_TM
    blk = ATTN_BLOCK
    per = tm // blk
    n_blocks = rows // blk
    prev_map = lambda i: (jnp.maximum(i * per - 1, 0), 0)
    next_map = lambda i: (jnp.minimum((i + 1) * per, n_blocks - 1), 0)
    mid_map = lambda i: (i, 0)
    kernel = functools.partial(_attn_core_kernel, tm=tm, seq=seq)
    return pl.pallas_call(
        kernel,
        name="attn_core",
        grid=(rows // tm,),
        in_specs=[
            pl.BlockSpec(memory_space=pltpu.SMEM),
            pl.BlockSpec((tm, Q_WIDTH), mid_map),
            pl.BlockSpec((blk, KV_WIDTH), prev_map),
            pl.BlockSpec((tm, KV_WIDTH), mid_map),
            pl.BlockSpec((blk, KV_WIDTH), next_map),
            pl.BlockSpec((blk, KV_WIDTH), prev_map),
            pl.BlockSpec((tm, KV_WIDTH), mid_map),
            pl.BlockSpec((blk, KV_WIDTH), next_map),
            pl.BlockSpec((tm, Q_WIDTH), mid_map),
        ],
        out_specs=pl.BlockSpec((tm, Q_WIDTH), mid_map),
        out_shape=jax.ShapeDtypeStruct((rows, Q_WIDTH), BF16),
        scratch_shapes=[
            pltpu.VMEM((tm + 2 * blk, KV_WIDTH), BF16),
            pltpu.VMEM((tm + 2 * blk, 2 * KV_WIDTH), BF16),
            pltpu.VMEM((4, GQA_GROUP * blk, blk), F32),
        ],
        compiler_params=pltpu.CompilerParams(
            dimension_semantics=("arbitrary",),
            vmem_limit_bytes=VMEM_LIMIT_BYTES),
    )(sink, q, k, k, k, v, v, v, zs)


def _rope_tables(seq):
    freq = ROPE_THETA ** (-jnp.arange(0, ROPE_DIM, 2, dtype=F32) / ROPE_DIM)
    pos = jnp.arange(seq, dtype=F32)
    ang = freq[:, None] * pos[None, :]
    cos = jnp.cos(ang)
    sin = jnp.sin(ang)
    rest = HEAD_DIM - ROPE_DIM
    cos_t = jnp.concatenate([cos, cos, jnp.ones((rest, seq), F32)], axis=0)
    sin_t = jnp.concatenate([-sin, sin, jnp.zeros((rest, seq), F32)], axis=0)
    return cos_t, sin_t


def _trunk(x, p, w, *, seq):
    batch = x.shape[0]
    rows = batch * seq
    x2d = x.reshape(rows, D_MODEL)
    p2d = p.reshape(p.shape[0] * rows, PLE_DIM)

    t = _pool_front(x2d, w["norm_g"][0:1], w["a_w_in"], w["a_w_grp"], w["a_scale"], seq=seq)
    x2d = _out_pe(t, w["a_w_out"], x2d, p2d, 0, w["pe_w_proj"][0], w["pe_norm_g"][0:1],
                  w["pe_w_gate"][0])

    q, k, v, zs = _attn_front(x2d, w["norm_g"][1:2], w["b_w_in"], w["rope_cos"], w["rope_sin"],
                              w["b_q_norm"], w["b_k_norm"], seq=seq)
    og = _attn_core(w["b_sink"], q, k, v, zs, seq=seq)
    x2d = _out_pe(og, w["b_w_out"], x2d, p2d, 1, w["pe_w_proj"][1],
                  w["pe_norm_g"][1:2], w["pe_w_gate"][1])
    return x2d.reshape(batch, seq, D_MODEL)


def kernel(x_prompt, x_sample, p_prompt, p_sample, norm_g, a_w_in, a_w_grp, a_scale, a_w_out,
           b_w_in, b_q_norm, b_k_norm, b_sink, b_w_out, pe_w_proj, pe_norm_g, pe_w_gate):
    w = {
        "norm_g": norm_g,
        "a_w_in": a_w_in[0].astype(BF16),
        "a_w_grp": a_w_grp[0].astype(BF16),
        "a_scale": a_scale,
        "a_w_out": a_w_out[0].astype(BF16),
        "b_w_in": b_w_in[0].astype(BF16),
        "b_q_norm": b_q_norm,
        "b_k_norm": b_k_norm,
        "b_sink": b_sink[0],
        "b_w_out": b_w_out[0].astype(BF16),
        "pe_w_proj": pe_w_proj.astype(BF16),
        "pe_norm_g": pe_norm_g,
        "pe_w_gate": pe_w_gate.astype(BF16),
    }
    w["rope_cos"], w["rope_sin"] = _rope_tables(max(x_prompt.shape[1], x_sample.shape[1]))
    y_prompt = _trunk(x_prompt, p_prompt, w, seq=x_prompt.shape[1])
    y_sample = _trunk(x_sample, p_sample, w, seq=x_sample.shape[1])
    return (y_prompt, y_sample)
```

```python
import functools
import math

import jax
import jax.numpy as jnp
from jax import lax
from jax.experimental import pallas as pl
from jax.experimental.pallas import tpu as pltpu

F32 = jnp.float32
BF16 = jnp.bfloat16

D_MODEL = 2048
PLE_DIM = 256
EPS = 1e-6

POOL_WIDTH = 4096
POOL_WINDOWS = (2, 4, 8, 16)
N_POOL_GROUPS = len(POOL_WINDOWS)
POOL_GROUP_WIDTH = POOL_WIDTH // N_POOL_GROUPS
POOL_HALO = 8

HEAD_DIM = 128
N_Q_HEADS = 16
N_KV_HEADS = 4
GQA_GROUP = N_Q_HEADS // N_KV_HEADS
Q_WIDTH = N_Q_HEADS * HEAD_DIM
KV_WIDTH = N_KV_HEADS * HEAD_DIM
ATTN_IN_WIDTH = 2 * Q_WIDTH + 2 * KV_WIDTH
ATTN_BLOCK = 128
ROPE_DIM = HEAD_DIM // 4
ROPE_THETA = 500000.0
NEG_BIG = -1e30
LOG2_E = math.log2(math.e)

VMEM_LIMIT_BYTES = 56 * 1024 * 1024
LANES = 128
SUBLANES = 8

POOL_TM = 512
POOL_GROUPS_PER_CALL = 2
ATTN_FRONT_TM = 512
ATTN_FRONT_TN = 1024
ATTN_CORE_TM = 512
OUT_TM_CHOICES = (512, 256)


def _rms_scale(x):
    return lax.rsqrt(jnp.mean(x * x, axis=-1, keepdims=True) + EPS)


def _dot(a, b):
    return jnp.dot(a, b, preferred_element_type=F32)


def _silu(z):
    return z * jax.nn.sigmoid(z)


def _pool_front_kernel(x_ref, xprev_ref, xnext_ref, gain_ref, wu_ref, wz_ref, wgrp_ref,
                       scale_ref, t_ref, hn_ref, a_ref, zs_ref, ext_ref, *, tm, seq, first_group,
                       n_groups):
    i = pl.program_id(0)
    n_ext = tm + 2 * POOL_HALO
    c = POOL_GROUP_WIDTH
    n_levels = first_group + n_groups
    t0 = lax.rem(i * tm, seq)
    has_prev = t0 > 0
    has_next = t0 + tm < seq
    x = x_ref[...]
    r = _rms_scale(x)
    xp = xprev_ref[...]
    rp = jnp.where(has_prev, _rms_scale(xp), 0.0)
    xn = xnext_ref[...]
    rn = jnp.where(has_next, _rms_scale(xn), 0.0)
    edge = 2 * SUBLANES
    row = lax.broadcasted_iota(jnp.int32, (edge, 1), 0)
    inv_first, inv_last = [], []
    for win in POOL_WINDOWS[:n_levels]:
        half = win // 2
        for first_pos, out in ((t0 + row, inv_first), (t0 + tm - edge + row, inv_last)):
            cnt = jnp.minimum(first_pos + half, seq) - jnp.maximum(first_pos - half, 0)
            out.append(1.0 / cnt.astype(F32))

    n_chunks = D_MODEL // LANES
    for ch in range(n_chunks):
        cols = slice(ch * LANES, (ch + 1) * LANES)
        hn_ref[:, cols] = (x_ref[:, cols] * r * gain_ref[:, cols]).astype(BF16)
    for k in range(n_groups):
        gcols = slice(k * c, (k + 1) * c)
        zs_ref[:, gcols] = _silu(_dot(hn_ref[...], wz_ref[:, gcols]))

    pad = POOL_HALO
    zeros = jnp.zeros((pad, LANES), F32)
    n_sets = ext_ref.shape[0]
    for s in range(n_sets):
        for buf in range(3):
            ext_ref[s, buf, 0:pad, :] = zeros
            ext_ref[s, buf, pad + n_ext:, :] = zeros
    centre = slice(pad + POOL_HALO, pad + POOL_HALO + tm)
    for ch in range(n_chunks):
        cols = slice(ch * LANES, (ch + 1) * LANES)
        ext = ext_ref.at[ch % n_sets]
        gain = gain_ref[:, cols]
        ext[0, pad:pad + POOL_HALO, :] = xprev_ref[:, cols] * rp * gain
        ext[0, centre, :] = x_ref[:, cols] * r * gain
        ext[0, pad + POOL_HALO + tm:pad + n_ext, :] = xnext_ref[:, cols] * rn * gain
        src = 0
        for gi, win in enumerate(POOL_WINDOWS[:n_levels]):
            lo, hi = (1, 0) if gi == 0 else (win // 4, win // 4)
            level = ext[src, pad - lo:pad - lo + n_ext, :] + ext[src, pad + hi:pad + hi + n_ext, :]
            if gi + 1 < n_levels:
                src = 1 + gi % 2
                ext[src, pad:pad + n_ext, :] = level
            if gi < first_group:
                continue
            k = gi - first_group
            wsum = level[POOL_HALO:POOL_HALO + tm]
            h = ext[0, centre, :]
            a_ref[k, :, cols] = (wsum * (1.0 / win) - h).astype(BF16)
            a_ref[k, 0:edge, cols] = (wsum[:edge] * inv_first[gi] - h[:edge]).astype(BF16)
            a_ref[k, tm - edge:tm, cols] = (wsum[tm - edge:] * inv_last[gi] - h[tm - edge:]
                                            ).astype(BF16)

    for k in range(n_groups):
        gcols = slice(k * c, (k + 1) * c)
        m = _dot(a_ref[k], wu_ref[:, gcols])
        mm = _dot(m.astype(BF16), wgrp_ref[k])
        t_ref[:, gcols] = ((mm * scale_ref[:, gcols]) * zs_ref[:, gcols]).astype(BF16)


def _pool_front(x2d, gain, w_in, w_grp, scale, *, seq, part):
    rows = x2d.shape[0]
    tm = POOL_TM
    n_groups = POOL_GROUPS_PER_CALL
    width = n_groups * POOL_GROUP_WIDTH
    n_parts = N_POOL_GROUPS // n_groups
    n_row_blocks8 = rows // SUBLANES
    resident = pl.Buffered(1)
    kernel = functools.partial(_pool_front_kernel, tm=tm, seq=seq, first_group=part * n_groups,
                               n_groups=n_groups)
    return pl.pallas_call(
        kernel,
        name="pool_front",
        grid=(rows // tm,),
        in_specs=[
            pl.BlockSpec((tm, D_MODEL), lambda i: (i, 0)),
            pl.BlockSpec((POOL_HALO, D_MODEL),
                         lambda i: (jnp.maximum(i * (tm // SUBLANES) - 1, 0), 0)),
            pl.BlockSpec((POOL_HALO, D_MODEL),
                         lambda i: (jnp.minimum((i + 1) * (tm // SUBLANES), n_row_blocks8 - 1), 0)),
            pl.BlockSpec((1, D_MODEL), lambda i: (0, 0)),
            pl.BlockSpec((D_MODEL, width), lambda i: (0, part), pipeline_mode=resident),
            pl.BlockSpec((D_MODEL, width), lambda i: (0, n_parts + part), pipeline_mode=resident),
            pl.BlockSpec((n_groups, POOL_GROUP_WIDTH, POOL_GROUP_WIDTH), lambda i: (part, 0, 0),
                         pipeline_mode=resident),
            pl.BlockSpec((1, width), lambda i: (0, part)),
        ],
        out_specs=pl.BlockSpec((tm, width), lambda i: (i, 0)),
        out_shape=jax.ShapeDtypeStruct((rows, width), BF16),
        scratch_shapes=[
            pltpu.VMEM((tm, D_MODEL), BF16),
            pltpu.VMEM((n_groups, tm, D_MODEL), BF16),
            pltpu.VMEM((tm, width), F32),
            pltpu.VMEM((2, 3, tm + 4 * POOL_HALO, LANES), F32),
        ],
        compiler_params=pltpu.CompilerParams(
            dimension_semantics=("arbitrary",),
            vmem_limit_bytes=VMEM_LIMIT_BYTES),
    )(x2d, x2d, x2d, gain, w_in, w_in, w_grp, scale)


def _out_pe_kernel(*refs, n_parts):
    t_refs = refs[:n_parts]
    wout_ref, x_ref, p_ref, wp_ref, gain_ref, wgate_ref, out_ref = refs[n_parts:]
    y = None
    k0 = 0
    for t_ref in t_refs:
        k1 = k0 + t_ref.shape[1]
        part = _dot(t_ref[...], wout_ref[k0:k1, :])
        y = part if y is None else y + part
        k0 = k1
    h = x_ref[...] + y
    hn = (h * _rms_scale(h) * gain_ref[...]).astype(BF16)
    gate = jax.nn.sigmoid(_dot(hn, wgate_ref[...]))
    e = _dot(p_ref[...].astype(BF16), wp_ref[...])
    out_ref[...] = h + e * gate


def _out_pe_tile(k):
    weights = 2 * (k * D_MODEL + PLE_DIM * D_MODEL + D_MODEL * D_MODEL)
    for tm in OUT_TM_CHOICES:
        row_blocks = 2 * tm * (2 * k + 4 * D_MODEL + 4 * PLE_DIM + 4 * D_MODEL)
        temporaries = 3 * tm * D_MODEL * 4
        if weights + row_blocks + temporaries <= VMEM_LIMIT_BYTES:
            return tm
    raise ValueError(f"no out_pe row tile fits VMEM for k={k}")


def _out_pe(t_parts, w_out, x2d, p2d, layer, w_proj, gain, w_gate):
    rows = x2d.shape[0]
    k = sum(t.shape[1] for t in t_parts)
    assert k == w_out.shape[0]
    tm = _out_pe_tile(k)
    p_block_offset = layer * (rows // tm)
    resident = pl.Buffered(1)
    return pl.pallas_call(
        functools.partial(_out_pe_kernel, n_parts=len(t_parts)),
        name="out_pe",
        grid=(rows // tm,),
        in_specs=[pl.BlockSpec((tm, t.shape[1]), lambda i: (i, 0)) for t in t_parts] + [
            pl.BlockSpec((k, D_MODEL), lambda i: (0, 0), pipeline_mode=resident),
            pl.BlockSpec((tm, D_MODEL), lambda i: (i, 0)),
            pl.BlockSpec((tm, PLE_DIM), lambda i: (i + p_block_offset, 0)),
            pl.BlockSpec((PLE_DIM, D_MODEL), lambda i: (0, 0), pipeline_mode=resident),
            pl.BlockSpec((1, D_MODEL), lambda i: (0, 0)),
            pl.BlockSpec((D_MODEL, D_MODEL), lambda i: (0, 0), pipeline_mode=resident),
        ],
        out_specs=pl.BlockSpec((tm, D_MODEL), lambda i: (i, 0)),
        out_shape=jax.ShapeDtypeStruct((rows, D_MODEL), F32),
        compiler_params=pltpu.CompilerParams(
            dimension_semantics=("arbitrary",),
            vmem_limit_bytes=VMEM_LIMIT_BYTES),
    )(*t_parts, w_out, x2d, p2d, w_proj, gain, w_gate)


def _attn_front_kernel(x_ref, gain_ref, w_ref, cos_ref, sin_ref, qg_ref, kg_ref,
                       q_ref, k_ref, v_ref, zs_ref, hn_ref):
    x = x_ref[...]
    hn_ref[...] = (x * _rms_scale(x) * gain_ref[...]).astype(BF16)
    cos = cos_ref[...].T
    sin = sin_ref[...].T
    half = ROPE_DIM // 2
    first_half = lax.broadcasted_iota(jnp.int32, cos.shape, 1) < half

    def norm_rope(slab, head_gain):
        s = slab * _rms_scale(slab) * head_gain
        partner = jnp.where(first_half, pltpu.roll(s, HEAD_DIM - half, axis=1),
                            pltpu.roll(s, half, axis=1))
        return s * cos + partner * sin

    tn = ATTN_FRONT_TN
    heads_per_block = tn // HEAD_DIM
    for n in range(ATTN_IN_WIDTH // tn):
        y = _dot(hn_ref[...], w_ref[:, n * tn:(n + 1) * tn])
        if n < Q_WIDTH // tn:
            for h in range(heads_per_block):
                cols = slice(h * HEAD_DIM, (h + 1) * HEAD_DIM)
                q_ref[:, n * tn + h * HEAD_DIM:n * tn + (h + 1) * HEAD_DIM] = (
                    norm_rope(y[:, cols], qg_ref[...]).astype(BF16))
        elif n == Q_WIDTH // tn:
            for h in range(N_KV_HEADS):
                cols = slice(h * HEAD_DIM, (h + 1) * HEAD_DIM)
                k_ref[:, cols] = norm_rope(y[:, cols], kg_ref[...]).astype(BF16)
            v_ref[...] = y[:, KV_WIDTH:].astype(BF16)
        else:
            z0 = (n - Q_WIDTH // tn - 1) * tn
            zs_ref[:, z0:z0 + tn] = _silu(y)


def _attn_front(x2d, gain, w_in, cos_t, sin_t, q_gain, k_gain, *, seq):
    rows = x2d.shape[0]
    tm = ATTN_FRONT_TM
    assert 2 * KV_WIDTH == ATTN_FRONT_TN
    seq_blocks = seq // tm
    table_spec = pl.BlockSpec((HEAD_DIM, tm), lambda i: (0, lax.rem(i, seq_blocks)))
    row_spec = lambda width: pl.BlockSpec((tm, width), lambda i: (i, 0))
    return pl.pallas_call(
        _attn_front_kernel,
        name="attn_front",
        grid=(rows // tm,),
        in_specs=[
            row_spec(D_MODEL),
            pl.BlockSpec((1, D_MODEL), lambda i: (0, 0)),
            pl.BlockSpec((D_MODEL, ATTN_IN_WIDTH), lambda i: (0, 0), pipeline_mode=pl.Buffered(1)),
            table_spec, table_spec,
            pl.BlockSpec((1, HEAD_DIM), lambda i: (0, 0)),
            pl.BlockSpec((1, HEAD_DIM), lambda i: (0, 0)),
        ],
        out_specs=[row_spec(Q_WIDTH), row_spec(KV_WIDTH), row_spec(KV_WIDTH), row_spec(Q_WIDTH)],
        out_shape=[
            jax.ShapeDtypeStruct((rows, Q_WIDTH), BF16),
            jax.ShapeDtypeStruct((rows, KV_WIDTH), BF16),
            jax.ShapeDtypeStruct((rows, KV_WIDTH), BF16),
            jax.ShapeDtypeStruct((rows, Q_WIDTH), F32),
        ],
        scratch_shapes=[pltpu.VMEM((tm, D_MODEL), BF16)],
        compiler_params=pltpu.CompilerParams(
            dimension_semantics=("arbitrary",),
            vmem_limit_bytes=VMEM_LIMIT_BYTES),
    )(x2d, gain, w_in, cos_t, sin_t, q_gain, k_gain)


def _attn_core_kernel(sink_ref, q_ref, kprev_ref, kmid_ref, knext_ref, vprev_ref, vmid_ref,
                      vnext_ref, zs_ref, o_ref, kbuf_ref, vbuf_ref, bias_ref, *, tm, seq):
    i = pl.program_id(0)
    blk = ATTN_BLOCK
    n_qb = tm // blk
    t0 = lax.rem(i * tm, seq)
    kbuf_ref[0:blk, :] = kprev_ref[...]
    kbuf_ref[blk:blk + tm, :] = kmid_ref[...]
    kbuf_ref[blk + tm:, :] = knext_ref[...]
    for h in range(N_KV_HEADS):
        src = slice(h * HEAD_DIM, (h + 1) * HEAD_DIM)
        dst = slice(2 * h * HEAD_DIM, (2 * h + 1) * HEAD_DIM)
        vbuf_ref[0:blk, dst] = vprev_ref[:, src]
        vbuf_ref[blk:blk + tm, dst] = vmid_ref[:, src]
        vbuf_ref[blk + tm:, dst] = vnext_ref[:, src]
        vbuf_ref[:, (2 * h + 1) * HEAD_DIM:(2 * h + 2) * HEAD_DIM] = jnp.ones(
            (tm + 2 * blk, HEAD_DIM), BF16)

    n_rows = GQA_GROUP * blk
    qi = lax.rem(lax.broadcasted_iota(jnp.int32, (n_rows, blk), 0), blk)
    kj = lax.broadcasted_iota(jnp.int32, (n_rows, blk), 1)
    prev_bias = jnp.where(kj >= qi, 0.0, NEG_BIG).astype(F32)
    next_bias = jnp.where(kj <= qi, 0.0, NEG_BIG).astype(F32)
    bias_ref[0] = jnp.where(t0 == 0, NEG_BIG, prev_bias)
    bias_ref[1] = prev_bias
    bias_ref[2] = jnp.where(t0 + tm == seq, NEG_BIG, next_bias)
    bias_ref[3] = next_bias

    sqrt_d = math.sqrt(HEAD_DIM)
    exp2_scale = LOG2_E / sqrt_d

    for qb in range(n_qb):
        row0 = qb * blk
        rows = pl.ds(row0, blk)
        b_prev = bias_ref[0 if qb == 0 else 1]
        b_next = bias_ref[2 if qb == n_qb - 1 else 3]
        for kvh in range(N_KV_HEADS):
            heads = [kvh * GQA_GROUP + hg for hg in range(GQA_GROUP)]
            qs = jnp.concatenate(
                [q_ref[rows, h * HEAD_DIM:(h + 1) * HEAD_DIM] for h in heads], axis=0)
            kw = kbuf_ref[pl.ds(row0, 3 * blk), kvh * HEAD_DIM:(kvh + 1) * HEAD_DIM]
            vw = vbuf_ref[pl.ds(row0, 3 * blk), 2 * kvh * HEAD_DIM:(2 * kvh + 2) * HEAD_DIM]
            s = lax.dot_general(qs, kw, (((1,), (1,)), ((), ())), preferred_element_type=F32)
            sb = (s[:, 0:blk] + b_prev, s[:, blk:2 * blk], s[:, 2 * blk:] + b_next)
            row_max = jnp.max(jnp.maximum(jnp.maximum(sb[0], sb[1]), sb[2]), axis=-1, keepdims=True)
            e_rows, m_rows, sinks = [], [], []
            for hg, h in enumerate(heads):
                r = slice(hg * blk, (hg + 1) * blk)
                sink_raw = sink_ref[h] * sqrt_d
                m_h = jnp.maximum(row_max[r], sink_raw)
                e_rows.append(jnp.concatenate(
                    [jnp.exp2((b[r] - m_h) * exp2_scale).astype(BF16) for b in sb], axis=1))
                m_rows.append(m_h)
                sinks.append(sink_raw)
            e = jnp.concatenate(e_rows, axis=0)
            ov = _dot(e, vw)
            for hg, h in enumerate(heads):
                r = slice(hg * blk, (hg + 1) * blk)
                hcols = slice(h * HEAD_DIM, (h + 1) * HEAD_DIM)
                denom = ov[r, HEAD_DIM:] + jnp.exp2((sinks[hg] - m_rows[hg]) * exp2_scale)
                o_ref[rows, hcols] = (ov[r, :HEAD_DIM] * (1.0 / denom) * zs_ref[rows, hcols]
                                      ).astype(BF16)


def _attn_core(sink, q, k, v, zs, *, seq):
    rows = q.shape[0]
    tm = ATTN_CORE_TM
    blk = ATTN_BLOCK
    per = tm // blk
    n_blocks = rows // blk
    prev_map = lambda i: (jnp.maximum(i * per - 1, 0), 0)
    next_map = lambda i: (jnp.minimum((i + 1) * per, n_blocks - 1), 0)
    mid_map = lambda i: (i, 0)
    kernel = functools.partial(_attn_core_kernel, tm=tm, seq=seq)
    return pl.pallas_call(
        kernel,
        name="attn_core",
        grid=(rows // tm,),
        in_specs=[
            pl.BlockSpec(memory_space=pltpu.SMEM),
            pl.BlockSpec((tm, Q_WIDTH), mid_map),
            pl.BlockSpec((blk, KV_WIDTH), prev_map),
            pl.BlockSpec((tm, KV_WIDTH), mid_map),
            pl.BlockSpec((blk, KV_WIDTH), next_map),
            pl.BlockSpec((blk, KV_WIDTH), prev_map),
            pl.BlockSpec((tm, KV_WIDTH), mid_map),
            pl.BlockSpec((blk, KV_WIDTH), next_map),
            pl.BlockSpec((tm, Q_WIDTH), mid_map),
        ],
        out_specs=pl.BlockSpec((tm, Q_WIDTH), mid_map),
        out_shape=jax.ShapeDtypeStruct((rows, Q_WIDTH), BF16),
        scratch_shapes=[
            pltpu.VMEM((tm + 2 * blk, KV_WIDTH), BF16),
            pltpu.VMEM((tm + 2 * blk, 2 * KV_WIDTH), BF16),
            pltpu.VMEM((4, GQA_GROUP * blk, blk), F32),
        ],
        compiler_params=pltpu.CompilerParams(
            dimension_semantics=("arbitrary",),
            vmem_limit_bytes=VMEM_LIMIT_BYTES),
    )(sink, q, k, k, k, v, v, v, zs)


def _rope_tables(seq):
    freq = ROPE_THETA ** (-jnp.arange(0, ROPE_DIM, 2, dtype=F32) / ROPE_DIM)
    pos = jnp.arange(seq, dtype=F32)
    ang = freq[:, None] * pos[None, :]
    cos = jnp.cos(ang)
    sin = jnp.sin(ang)
    rest = HEAD_DIM - ROPE_DIM
    cos_t = jnp.concatenate([cos, cos, jnp.ones((rest, seq), F32)], axis=0)
    sin_t = jnp.concatenate([-sin, sin, jnp.zeros((rest, seq), F32)], axis=0)
    return cos_t, sin_t


def _trunk(x, p, w, *, seq):
    batch = x.shape[0]
    rows = batch * seq
    x2d = x.reshape(rows, D_MODEL)
    p2d = p.reshape(p.shape[0] * rows, PLE_DIM)

    t_parts = [_pool_front(x2d, w["norm_g"][0:1], w["a_w_in"], w["a_w_grp"], w["a_scale"], seq=seq,
                           part=part) for part in range(N_POOL_GROUPS // POOL_GROUPS_PER_CALL)]
    x2d = _out_pe(t_parts, w["a_w_out"], x2d, p2d, 0, w["pe_w_proj"][0], w["pe_norm_g"][0:1],
                  w["pe_w_gate"][0])

    q, k, v, zs = _attn_front(x2d, w["norm_g"][1:2], w["b_w_in"], w["rope_cos"], w["rope_sin"],
                              w["b_q_norm"], w["b_k_norm"], seq=seq)
    og = _attn_core(w["b_sink"], q, k, v, zs, seq=seq)
    x2d = _out_pe([og], w["b_w_out"], x2d, p2d, 1, w["pe_w_proj"][1],
                  w["pe_norm_g"][1:2], w["pe_w_gate"][1])
    return x2d.reshape(batch, seq, D_MODEL)


def kernel(x_prompt, x_sample, p_prompt, p_sample, norm_g, a_w_in, a_w_grp, a_scale, a_w_out,
           b_w_in, b_q_norm, b_k_norm, b_sink, b_w_out, pe_w_proj, pe_norm_g, pe_w_gate):
    w = {
        "norm_g": norm_g,
        "a_w_in": a_w_in[0].astype(BF16),
        "a_w_grp": a_w_grp[0].astype(BF16),
        "a_scale": a_scale,
        "a_w_out": a_w_out[0].astype(BF16),
        "b_w_in": b_w_in[0].astype(BF16),
        "b_q_norm": b_q_norm,
        "b_k_norm": b_k_norm,
        "b_sink": b_sink[0],
        "b_w_out": b_w_out[0].astype(BF16),
        "pe_w_proj": pe_w_proj.astype(BF16),
        "pe_norm_g": pe_norm_g,
        "pe_w_gate": pe_w_gate.astype(BF16),
    }
    w["rope_cos"], w["rope_sin"] = _rope_tables(max(x_prompt.shape[1], x_sample.shape[1]))
    y_prompt = _trunk(x_prompt, p_prompt, w, seq=x_prompt.shape[1])
    y_sample = _trunk(x_sample, p_sample, w, seq=x_sample.shape[1])
    return (y_prompt, y_sample)
```

```python
import functools
import math

import jax
import jax.numpy as jnp
from jax import lax
from jax.experimental import pallas as pl
from jax.experimental.pallas import tpu as pltpu

F32 = jnp.float32
BF16 = jnp.bfloat16

D_MODEL = 2048
PLE_DIM = 256
EPS = 1e-6

POOL_WIDTH = 4096
POOL_WINDOWS = (2, 4, 8, 16)
N_POOL_GROUPS = len(POOL_WINDOWS)
POOL_GROUP_WIDTH = POOL_WIDTH // N_POOL_GROUPS
POOL_HALO = 8

HEAD_DIM = 128
N_Q_HEADS = 16
N_KV_HEADS = 4
GQA_GROUP = N_Q_HEADS // N_KV_HEADS
Q_WIDTH = N_Q_HEADS * HEAD_DIM
KV_WIDTH = N_KV_HEADS * HEAD_DIM
ATTN_IN_WIDTH = 2 * Q_WIDTH + 2 * KV_WIDTH
ATTN_BLOCK = 128
ROPE_DIM = HEAD_DIM // 4
ROPE_THETA = 500000.0
NEG_BIG = -1e30
LOG2_E = math.log2(math.e)

VMEM_LIMIT_BYTES = 56 * 1024 * 1024
LANES = 128
SUBLANES = 8

POOL_TM = 512
POOL_GROUPS_PER_CALL = 2
POOL_ROW_BLOCK = 64
POOL_DIRECT_MAX = 8
ATTN_FRONT_TM = 512
ATTN_FRONT_TN = 1024
ATTN_CORE_TM = 512
OUT_TM_CHOICES = (512, 256)


def _rms_scale(x):
    return lax.rsqrt(jnp.mean(x * x, axis=-1, keepdims=True) + EPS)


def _dot(a, b):
    return jnp.dot(a, b, preferred_element_type=F32)


def _silu(z):
    return z * jax.nn.sigmoid(z)


def _fold_group_kernel(wu_ref, wgrp_ref, out_ref):
    out_ref[...] = _dot(wu_ref[...], wgrp_ref[0]).astype(BF16)


def _fold_group_proj(w_in, w_grp):
    c = POOL_GROUP_WIDTH
    return pl.pallas_call(
        _fold_group_kernel,
        name="fold_group_proj",
        grid=(N_POOL_GROUPS,),
        in_specs=[
            pl.BlockSpec((D_MODEL, c), lambda g: (0, g)),
            pl.BlockSpec((1, c, c), lambda g: (g, 0, 0)),
        ],
        out_specs=pl.BlockSpec((D_MODEL, c), lambda g: (0, g)),
        out_shape=jax.ShapeDtypeStruct((D_MODEL, POOL_WIDTH), BF16),
        compiler_params=pltpu.CompilerParams(
            dimension_semantics=("arbitrary",),
            vmem_limit_bytes=VMEM_LIMIT_BYTES),
    )(w_in, w_grp)


def _pool_front_kernel(x_ref, xprev_ref, xnext_ref, gain_ref, wug_ref, wz_ref,
                       scale_ref, t_ref, hn_ref, a_ref, zs_ref, h_ref, lvl_ref, *, tm, seq,
                       first_group, n_groups):
    i = pl.program_id(0)
    n_ext = tm + 2 * POOL_HALO
    c = POOL_GROUP_WIDTH
    n_levels = first_group + n_groups
    t0 = lax.rem(i * tm, seq)
    has_prev = t0 > 0
    has_next = t0 + tm < seq
    x = x_ref[...]
    r = _rms_scale(x)
    xp = xprev_ref[...]
    rp = jnp.where(has_prev, _rms_scale(xp), 0.0)
    xn = xnext_ref[...]
    rn = jnp.where(has_next, _rms_scale(xn), 0.0)
    edge = 2 * SUBLANES
    row = lax.broadcasted_iota(jnp.int32, (edge, 1), 0)
    inv_first, inv_last = [], []
    for win in POOL_WINDOWS[:n_levels]:
        half = win // 2
        for first_pos, out in ((t0 + row, inv_first), (t0 + tm - edge + row, inv_last)):
            cnt = jnp.minimum(first_pos + half, seq) - jnp.maximum(first_pos - half, 0)
            out.append(1.0 / cnt.astype(F32))

    pad = POOL_HALO
    top = pad + POOL_HALO
    n_buf = n_ext + 2 * pad
    n_chunks = D_MODEL // LANES
    for ch in range(n_chunks):
        cols = slice(ch * LANES, (ch + 1) * LANES)
        gain = gain_ref[:, cols]
        hc = x_ref[:, cols] * r * gain
        hn_ref[:, cols] = hc.astype(BF16)
        h_ref[top:top + tm, cols] = hc
        h_ref[pad:top, cols] = xprev_ref[:, cols] * rp * gain
        h_ref[top + tm:top + tm + POOL_HALO, cols] = xnext_ref[:, cols] * rn * gain
    h_ref[0:pad, :] = jnp.zeros((pad, D_MODEL), F32)
    h_ref[n_buf - pad:, :] = jnp.zeros((pad, D_MODEL), F32)
    for k in range(n_groups):
        gcols = slice(k * c, (k + 1) * c)
        zs_ref[:, gcols] = _silu(_dot(hn_ref[...], wz_ref[:, gcols]))

    centre_blocks = [(top + b, top + b + POOL_ROW_BLOCK) for b in range(0, tm, POOL_ROW_BLOCK)]
    all_blocks = [(pad, top)] + centre_blocks + [(top + tm, top + tm + POOL_HALO)]
    wins = POOL_WINDOWS[first_group:first_group + n_groups]
    n_sets = lvl_ref.shape[0]
    for ch in range(n_chunks):
        cols = slice(ch * LANES, (ch + 1) * LANES)
        lvl = lvl_ref.at[ch % n_sets]
        for k, win in enumerate(wins):
            gi = first_group + k
            feeds_next = k + 1 < n_groups and wins[k + 1] > POOL_DIRECT_MAX
            for b0, b1 in (all_blocks if feeds_next else centre_blocks):
                if win <= POOL_DIRECT_MAX:
                    terms = [h_ref[b0 + u:b1 + u, cols] for u in range(-(win // 2), win // 2)]
                    while len(terms) > 1:
                        terms = [terms[j] + terms[j + 1] for j in range(0, len(terms), 2)]
                    level = terms[0]
                else:
                    assert win == 2 * wins[k - 1] and wins[k - 1] <= POOL_DIRECT_MAX
                    level = lvl[b0 - win // 4:b1 - win // 4, :] + lvl[b0 + win // 4:b1 + win // 4, :]
                if feeds_next:
                    lvl[b0:b1, :] = level
                r0, r1 = b0 - top, b1 - top
                if r0 < 0 or r1 > tm:
                    continue
                h = h_ref[b0:b1, cols]
                inv = 1.0 / win
                if r0 == 0:
                    a_ref[k, 0:edge, cols] = (level[:edge] * inv_first[gi] - h[:edge]).astype(BF16)
                    a_ref[k, edge:r1, cols] = (level[edge:] * inv - h[edge:]).astype(BF16)
                elif r1 == tm:
                    a_ref[k, r0:tm - edge, cols] = (level[:-edge] * inv - h[:-edge]).astype(BF16)
                    a_ref[k, tm - edge:tm, cols] = (level[-edge:] * inv_last[gi] - h[-edge:]
                                                    ).astype(BF16)
                else:
                    a_ref[k, r0:r1, cols] = (level * inv - h).astype(BF16)

    for k in range(n_groups):
        gcols = slice(k * c, (k + 1) * c)
        mm = _dot(a_ref[k], wug_ref[:, gcols])
        t_ref[:, gcols] = ((mm * scale_ref[:, gcols]) * zs_ref[:, gcols]).astype(BF16)


def _pool_front(x2d, gain, w_ug, w_in, scale, *, seq, part):
    rows = x2d.shape[0]
    tm = POOL_TM
    n_groups = POOL_GROUPS_PER_CALL
    width = n_groups * POOL_GROUP_WIDTH
    n_parts = N_POOL_GROUPS // n_groups
    n_row_blocks8 = rows // SUBLANES
    resident = pl.Buffered(1)
    kernel = functools.partial(_pool_front_kernel, tm=tm, seq=seq, first_group=part * n_groups,
                               n_groups=n_groups)
    return pl.pallas_call(
        kernel,
        name="pool_front",
        grid=(rows // tm,),
        in_specs=[
            pl.BlockSpec((tm, D_MODEL), lambda i: (i, 0)),
            pl.BlockSpec((POOL_HALO, D_MODEL),
                         lambda i: (jnp.maximum(i * (tm // SUBLANES) - 1, 0), 0)),
            pl.BlockSpec((POOL_HALO, D_MODEL),
                         lambda i: (jnp.minimum((i + 1) * (tm // SUBLANES), n_row_blocks8 - 1), 0)),
            pl.BlockSpec((1, D_MODEL), lambda i: (0, 0)),
            pl.BlockSpec((D_MODEL, width), lambda i: (0, part), pipeline_mode=resident),
            pl.BlockSpec((D_MODEL, width), lambda i: (0, n_parts + part), pipeline_mode=resident),
            pl.BlockSpec((1, width), lambda i: (0, part)),
        ],
        out_specs=pl.BlockSpec((tm, width), lambda i: (i, 0)),
        out_shape=jax.ShapeDtypeStruct((rows, width), BF16),
        scratch_shapes=[
            pltpu.VMEM((tm, D_MODEL), BF16),
            pltpu.VMEM((n_groups, tm, D_MODEL), BF16),
            pltpu.VMEM((tm, width), F32),
            pltpu.VMEM((tm + 4 * POOL_HALO, D_MODEL), F32),
            pltpu.VMEM((2, tm + 4 * POOL_HALO, LANES), F32),
        ],
        compiler_params=pltpu.CompilerParams(
            dimension_semantics=("arbitrary",),
            vmem_limit_bytes=VMEM_LIMIT_BYTES),
    )(x2d, x2d, x2d, gain, w_ug, w_in, scale)


def _out_pe_kernel(*refs, n_parts):
    t_refs = refs[:n_parts]
    wout_ref, x_ref, p_ref, wp_ref, gain_ref, wgate_ref, out_ref = refs[n_parts:]
    y = None
    k0 = 0
    for t_ref in t_refs:
        k1 = k0 + t_ref.shape[1]
        part = _dot(t_ref[...], wout_ref[k0:k1, :])
        y = part if y is None else y + part
        k0 = k1
    h = x_ref[...] + y
    hn = (h * _rms_scale(h) * gain_ref[...]).astype(BF16)
    gate = jax.nn.sigmoid(_dot(hn, wgate_ref[...]))
    e = _dot(p_ref[...].astype(BF16), wp_ref[...])
    out_ref[...] = h + e * gate


def _out_pe_tile(k):
    weights = 2 * (k * D_MODEL + PLE_DIM * D_MODEL + D_MODEL * D_MODEL)
    for tm in OUT_TM_CHOICES:
        row_blocks = 2 * tm * (2 * k + 4 * D_MODEL + 4 * PLE_DIM + 4 * D_MODEL)
        temporaries = 3 * tm * D_MODEL * 4
        if weights + row_blocks + temporaries <= VMEM_LIMIT_BYTES:
            return tm
    raise ValueError(f"no out_pe row tile fits VMEM for k={k}")


def _out_pe(t_parts, w_out, x2d, p2d, layer, w_proj, gain, w_gate):
    rows = x2d.shape[0]
    k = sum(t.shape[1] for t in t_parts)
    assert k == w_out.shape[0]
    tm = _out_pe_tile(k)
    p_block_offset = layer * (rows // tm)
    resident = pl.Buffered(1)
    return pl.pallas_call(
        functools.partial(_out_pe_kernel, n_parts=len(t_parts)),
        name="out_pe",
        grid=(rows // tm,),
        in_specs=[pl.BlockSpec((tm, t.shape[1]), lambda i: (i, 0)) for t in t_parts] + [
            pl.BlockSpec((k, D_MODEL), lambda i: (0, 0), pipeline_mode=resident),
            pl.BlockSpec((tm, D_MODEL), lambda i: (i, 0)),
            pl.BlockSpec((tm, PLE_DIM), lambda i: (i + p_block_offset, 0)),
            pl.BlockSpec((PLE_DIM, D_MODEL), lambda i: (0, 0), pipeline_mode=resident),
            pl.BlockSpec((1, D_MODEL), lambda i: (0, 0)),
            pl.BlockSpec((D_MODEL, D_MODEL), lambda i: (0, 0), pipeline_mode=resident),
        ],
        out_specs=pl.BlockSpec((tm, D_MODEL), lambda i: (i, 0)),
        out_shape=jax.ShapeDtypeStruct((rows, D_MODEL), F32),
        compiler_params=pltpu.CompilerParams(
            dimension_semantics=("arbitrary",),
            vmem_limit_bytes=VMEM_LIMIT_BYTES),
    )(*t_parts, w_out, x2d, p2d, w_proj, gain, w_gate)


def _attn_front_kernel(x_ref, gain_ref, w_ref, cos_ref, sin_ref, qg_ref, kg_ref,
                       q_ref, k_ref, v_ref, zs_ref, hn_ref):
    x = x_ref[...]
    hn_ref[...] = (x * _rms_scale(x) * gain_ref[...]).astype(BF16)
    cos = cos_ref[...].T
    sin = sin_ref[...].T
    half = ROPE_DIM // 2
    first_half = lax.broadcasted_iota(jnp.int32, cos.shape, 1) < half

    def norm_rope(slab, head_gain):
        s = slab * _rms_scale(slab) * head_gain
        partner = jnp.where(first_half, pltpu.roll(s, HEAD_DIM - half, axis=1),
                            pltpu.roll(s, half, axis=1))
        return s * cos + partner * sin

    tn = ATTN_FRONT_TN
    heads_per_block = tn // HEAD_DIM
    for n in range(ATTN_IN_WIDTH // tn):
        y = _dot(hn_ref[...], w_ref[:, n * tn:(n + 1) * tn])
        if n < Q_WIDTH // tn:
            for h in range(heads_per_block):
                cols = slice(h * HEAD_DIM, (h + 1) * HEAD_DIM)
                q_ref[:, n * tn + h * HEAD_DIM:n * tn + (h + 1) * HEAD_DIM] = (
                    norm_rope(y[:, cols], qg_ref[...]).astype(BF16))
        elif n == Q_WIDTH // tn:
            for h in range(N_KV_HEADS):
                cols = slice(h * HEAD_DIM, (h + 1) * HEAD_DIM)
                k_ref[:, cols] = norm_rope(y[:, cols], kg_ref[...]).astype(BF16)
            v_ref[...] = y[:, KV_WIDTH:].astype(BF16)
        else:
            z0 = (n - Q_WIDTH // tn - 1) * tn
            zs_ref[:, z0:z0 + tn] = _silu(y)


def _attn_front(x2d, gain, w_in, cos_t, sin_t, q_gain, k_gain, *, seq):
    rows = x2d.shape[0]
    tm = ATTN_FRONT_TM
    assert 2 * KV_WIDTH == ATTN_FRONT_TN
    seq_blocks = seq // tm
    table_spec = pl.BlockSpec((HEAD_DIM, tm), lambda i: (0, lax.rem(i, seq_blocks)))
    row_spec = lambda width: pl.BlockSpec((tm, width), lambda i: (i, 0))
    return pl.pallas_call(
        _attn_front_kernel,
        name="attn_front",
        grid=(rows // tm,),
        in_specs=[
            row_spec(D_MODEL),
            pl.BlockSpec((1, D_MODEL), lambda i: (0, 0)),
            pl.BlockSpec((D_MODEL, ATTN_IN_WIDTH), lambda i: (0, 0), pipeline_mode=pl.Buffered(1)),
            table_spec, table_spec,
            pl.BlockSpec((1, HEAD_DIM), lambda i: (0, 0)),
            pl.BlockSpec((1, HEAD_DIM), lambda i: (0, 0)),
        ],
        out_specs=[row_spec(Q_WIDTH), row_spec(KV_WIDTH), row_spec(KV_WIDTH), row_spec(Q_WIDTH)],
        out_shape=[
            jax.ShapeDtypeStruct((rows, Q_WIDTH), BF16),
            jax.ShapeDtypeStruct((rows, KV_WIDTH), BF16),
            jax.ShapeDtypeStruct((rows, KV_WIDTH), BF16),
            jax.ShapeDtypeStruct((rows, Q_WIDTH), F32),
        ],
        scratch_shapes=[pltpu.VMEM((tm, D_MODEL), BF16)],
        compiler_params=pltpu.CompilerParams(
            dimension_semantics=("arbitrary",),
            vmem_limit_bytes=VMEM_LIMIT_BYTES),
    )(x2d, gain, w_in, cos_t, sin_t, q_gain, k_gain)


def _attn_core_kernel(sink_ref, q_ref, kprev_ref, kmid_ref, knext_ref, vprev_ref, vmid_ref,
                      vnext_ref, zs_ref, o_ref, kbuf_ref, vbuf_ref, bias_ref, *, tm, seq):
    i = pl.program_id(0)
    blk = ATTN_BLOCK
    n_qb = tm // blk
    t0 = lax.rem(i * tm, seq)
    kbuf_ref[0:blk, :] = kprev_ref[...]
    kbuf_ref[blk:blk + tm, :] = kmid_ref[...]
    kbuf_ref[blk + tm:, :] = knext_ref[...]
    for h in range(N_KV_HEADS):
        src = slice(h * HEAD_DIM, (h + 1) * HEAD_DIM)
        dst = slice(2 * h * HEAD_DIM, (2 * h + 1) * HEAD_DIM)
        vbuf_ref[0:blk, dst] = vprev_ref[:, src]
        vbuf_ref[blk:blk + tm, dst] = vmid_ref[:, src]
        vbuf_ref[blk + tm:, dst] = vnext_ref[:, src]
        vbuf_ref[:, (2 * h + 1) * HEAD_DIM:(2 * h + 2) * HEAD_DIM] = jnp.ones(
            (tm + 2 * blk, HEAD_DIM), BF16)

    n_rows = GQA_GROUP * blk
    qi = lax.rem(lax.broadcasted_iota(jnp.int32, (n_rows, blk), 0), blk)
    kj = lax.broadcasted_iota(jnp.int32, (n_rows, blk), 1)
    prev_bias = jnp.where(kj >= qi, 0.0, NEG_BIG).astype(F32)
    next_bias = jnp.where(kj <= qi, 0.0, NEG_BIG).astype(F32)
    bias_ref[0] = jnp.where(t0 == 0, NEG_BIG, prev_bias)
    bias_ref[1] = prev_bias
    bias_ref[2] = jnp.where(t0 + tm == seq, NEG_BIG, next_bias)
    bias_ref[3] = next_bias

    sqrt_d = math.sqrt(HEAD_DIM)
    exp2_scale = LOG2_E / sqrt_d

    for qb in range(n_qb):
        row0 = qb * blk
        rows = pl.ds(row0, blk)
        b_prev = bias_ref[0 if qb == 0 else 1]
        b_next = bias_ref[2 if qb == n_qb - 1 else 3]
        for kvh in range(N_KV_HEADS):
            heads = [kvh * GQA_GROUP + hg for hg in range(GQA_GROUP)]
            qs = jnp.concatenate(
                [q_ref[rows, h * HEAD_DIM:(h + 1) * HEAD_DIM] for h in heads], axis=0)
            kw = kbuf_ref[pl.ds(row0, 3 * blk), kvh * HEAD_DIM:(kvh + 1) * HEAD_DIM]
            vw = vbuf_ref[pl.ds(row0, 3 * blk), 2 * kvh * HEAD_DIM:(2 * kvh + 2) * HEAD_DIM]
            s = lax.dot_general(qs, kw, (((1,), (1,)), ((), ())), preferred_element_type=F32)
            sb = (s[:, 0:blk] + b_prev, s[:, blk:2 * blk], s[:, 2 * blk:] + b_next)
            row_max = jnp.max(jnp.maximum(jnp.maximum(sb[0], sb[1]), sb[2]), axis=-1, keepdims=True)
            e_rows, m_rows, sinks = [], [], []
            for hg, h in enumerate(heads):
                r = slice(hg * blk, (hg + 1) * blk)
                sink_raw = sink_ref[h] * sqrt_d
                m_h = jnp.maximum(row_max[r], sink_raw)
                e_rows.append(jnp.concatenate(
                    [jnp.exp2((b[r] - m_h) * exp2_scale).astype(BF16) for b in sb], axis=1))
                m_rows.append(m_h)
                sinks.append(sink_raw)
            e = jnp.concatenate(e_rows, axis=0)
            ov = _dot(e, vw)
            for hg, h in enumerate(heads):
                r = slice(hg * blk, (hg + 1) * blk)
                hcols = slice(h * HEAD_DIM, (h + 1) * HEAD_DIM)
                denom = ov[r, HEAD_DIM:] + jnp.exp2((sinks[hg] - m_rows[hg]) * exp2_scale)
                o_ref[rows, hcols] = (ov[r, :HEAD_DIM] * (1.0 / denom) * zs_ref[rows, hcols]
                                      ).astype(BF16)


def _attn_core(sink, q, k, v, zs, *, seq):
    rows = q.shape[0]
    tm = ATTN_CORE_TM
    blk = ATTN_BLOCK
    per = tm // blk
    n_blocks = rows // blk
    prev_map = lambda i: (jnp.maximum(i * per - 1, 0), 0)
    next_map = lambda i: (jnp.minimum((i + 1) * per, n_blocks - 1), 0)
    mid_map = lambda i: (i, 0)
    kernel = functools.partial(_attn_core_kernel, tm=tm, seq=seq)
    return pl.pallas_call(
        kernel,
        name="attn_core",
        grid=(rows // tm,),
        in_specs=[
            pl.BlockSpec(memory_space=pltpu.SMEM),
            pl.BlockSpec((tm, Q_WIDTH), mid_map),
            pl.BlockSpec((blk, KV_WIDTH), prev_map),
            pl.BlockSpec((tm, KV_WIDTH), mid_map),
            pl.BlockSpec((blk, KV_WIDTH), next_map),
            pl.BlockSpec((blk, KV_WIDTH), prev_map),
            pl.BlockSpec((tm, KV_WIDTH), mid_map),
            pl.BlockSpec((blk, KV_WIDTH), next_map),
            pl.BlockSpec((tm, Q_WIDTH), mid_map),
        ],
        out_specs=pl.BlockSpec((tm, Q_WIDTH), mid_map),
        out_shape=jax.ShapeDtypeStruct((rows, Q_WIDTH), BF16),
        scratch_shapes=[
            pltpu.VMEM((tm + 2 * blk, KV_WIDTH), BF16),
            pltpu.VMEM((tm + 2 * blk, 2 * KV_WIDTH), BF16),
            pltpu.VMEM((4, GQA_GROUP * blk, blk), F32),
        ],
        compiler_params=pltpu.CompilerParams(
            dimension_semantics=("arbitrary",),
            vmem_limit_bytes=VMEM_LIMIT_BYTES),
    )(sink, q, k, k, k, v, v, v, zs)


def _rope_tables(seq):
    freq = ROPE_THETA ** (-jnp.arange(0, ROPE_DIM, 2, dtype=F32) / ROPE_DIM)
    pos = jnp.arange(seq, dtype=F32)
    ang = freq[:, None] * pos[None, :]
    cos = jnp.cos(ang)
    sin = jnp.sin(ang)
    rest = HEAD_DIM - ROPE_DIM
    cos_t = jnp.concatenate([cos, cos, jnp.ones((rest, seq), F32)], axis=0)
    sin_t = jnp.concatenate([-sin, sin, jnp.zeros((rest, seq), F32)], axis=0)
    return cos_t, sin_t


def _trunk(x, p, w, *, seq):
    batch = x.shape[0]
    rows = batch * seq
    x2d = x.reshape(rows, D_MODEL)
    p2d = p.reshape(p.shape[0] * rows, PLE_DIM)

    t_parts = [_pool_front(x2d, w["norm_g"][0:1], w["a_w_ug"], w["a_w_in"], w["a_scale"], seq=seq,
                           part=part) for part in range(N_POOL_GROUPS // POOL_GROUPS_PER_CALL)]
    x2d = _out_pe(t_parts, w["a_w_out"], x2d, p2d, 0, w["pe_w_proj"][0], w["pe_norm_g"][0:1],
                  w["pe_w_gate"][0])

    q, k, v, zs = _attn_front(x2d, w["norm_g"][1:2], w["b_w_in"], w["rope_cos"], w["rope_sin"],
                              w["b_q_norm"], w["b_k_norm"], seq=seq)
    og = _attn_core(w["b_sink"], q, k, v, zs, seq=seq)
    x2d = _out_pe([og], w["b_w_out"], x2d, p2d, 1, w["pe_w_proj"][1],
                  w["pe_norm_g"][1:2], w["pe_w_gate"][1])
    return x2d.reshape(batch, seq, D_MODEL)


def kernel(x_prompt, x_sample, p_prompt, p_sample, norm_g, a_w_in, a_w_grp, a_scale, a_w_out,
           b_w_in, b_q_norm, b_k_norm, b_sink, b_w_out, pe_w_proj, pe_norm_g, pe_w_gate):
    w = {
        "norm_g": norm_g,
        "a_w_in": a_w_in[0].astype(BF16),
        "a_w_grp": a_w_grp[0].astype(BF16),
        "a_scale": a_scale,
        "a_w_out": a_w_out[0].astype(BF16),
        "b_w_in": b_w_in[0].astype(BF16),
        "b_q_norm": b_q_norm,
        "b_k_norm": b_k_norm,
        "b_sink": b_sink[0],
        "b_w_out": b_w_out[0].astype(BF16),
        "pe_w_proj": pe_w_proj.astype(BF16),
        "pe_norm_g": pe_norm_g,
        "pe_w_gate": pe_w_gate.astype(BF16),
    }
    w["a_w_ug"] = _fold_group_proj(w["a_w_in"], w["a_w_grp"])
    w["rope_cos"], w["rope_sin"] = _rope_tables(max(x_prompt.shape[1], x_sample.shape[1]))
    y_prompt = _trunk(x_prompt, p_prompt, w, seq=x_prompt.shape[1])
    y_sample = _trunk(x_sample, p_sample, w, seq=x_sample.shape[1])
    return (y_prompt, y_sample)
```

```python
import functools
import math

import jax
import jax.numpy as jnp
from jax import lax
from jax.experimental import pallas as pl
from jax.experimental.pallas import tpu as pltpu

F32 = jnp.float32
BF16 = jnp.bfloat16

D_MODEL = 2048
PLE_DIM = 256
EPS = 1e-6

POOL_WIDTH = 4096
POOL_WINDOWS = (2, 4, 8, 16)
N_POOL_GROUPS = len(POOL_WINDOWS)
POOL_GROUP_WIDTH = POOL_WIDTH // N_POOL_GROUPS
POOL_HALO = 8

HEAD_DIM = 128
N_Q_HEADS = 16
N_KV_HEADS = 4
GQA_GROUP = N_Q_HEADS // N_KV_HEADS
Q_WIDTH = N_Q_HEADS * HEAD_DIM
KV_WIDTH = N_KV_HEADS * HEAD_DIM
ATTN_IN_WIDTH = 2 * Q_WIDTH + 2 * KV_WIDTH
ATTN_BLOCK = 128
ROPE_DIM = HEAD_DIM // 4
ROPE_THETA = 500000.0
NEG_BIG = -1e30
LOG2_E = math.log2(math.e)

VMEM_LIMIT_BYTES = 56 * 1024 * 1024
LANES = 128
SUBLANES = 8

POOL_TM = 512
POOL_GROUPS_PER_CALL = 2
POOL_ROW_BLOCK = 64
POOL_DIRECT_MAX = 8
ATTN_FRONT_TM = 512
ATTN_FRONT_TN = 1024
ATTN_CORE_TM = 512
OUT_TM_CHOICES = (512, 256)


def _rms_scale(x):
    return lax.rsqrt(jnp.mean(x * x, axis=-1, keepdims=True) + EPS)


def _dot(a, b):
    return jnp.dot(a, b, preferred_element_type=F32)


def _silu(z):
    return z * jax.nn.sigmoid(z)


def _tile_position(row0, group_rows, group_seqs):
    start = sum(group_rows[:-1])
    t0 = lax.rem(row0 - start, group_seqs[-1])
    seq = group_seqs[-1]
    for rows, s in zip(group_rows[-2::-1], group_seqs[-2::-1]):
        start -= rows
        inside = row0 < start + rows
        t0 = jnp.where(inside, lax.rem(row0 - start, s), t0)
        seq = jnp.where(inside, s, seq)
    return t0, seq


def _fold_group_kernel(wu_ref, wgrp_ref, out_ref):
    out_ref[...] = _dot(wu_ref[...].astype(BF16), wgrp_ref[0].astype(BF16)).astype(BF16)


def _fold_group_proj(w_in, w_grp):
    c = POOL_GROUP_WIDTH
    return pl.pallas_call(
        _fold_group_kernel,
        name="fold_group_proj",
        grid=(N_POOL_GROUPS,),
        in_specs=[
            pl.BlockSpec((D_MODEL, c), lambda g: (0, g)),
            pl.BlockSpec((1, c, c), lambda g: (g, 0, 0)),
        ],
        out_specs=pl.BlockSpec((D_MODEL, c), lambda g: (0, g)),
        out_shape=jax.ShapeDtypeStruct((D_MODEL, POOL_WIDTH), BF16),
        compiler_params=pltpu.CompilerParams(
            dimension_semantics=("arbitrary",),
            vmem_limit_bytes=VMEM_LIMIT_BYTES),
    )(w_in, w_grp)


def _pool_front_kernel(x_ref, xprev_ref, xnext_ref, gain_ref, wug_ref, wz_ref,
                       scale_ref, t_ref, hn_ref, a_ref, zs_ref, h_ref, lvl_ref, *, tm, seq,
                       first_group, n_groups):
    i = pl.program_id(0)
    n_ext = tm + 2 * POOL_HALO
    c = POOL_GROUP_WIDTH
    n_levels = first_group + n_groups
    t0 = lax.rem(i * tm, seq)
    has_prev = t0 > 0
    has_next = t0 + tm < seq
    x = x_ref[...]
    r = _rms_scale(x)
    xp = xprev_ref[...]
    rp = jnp.where(has_prev, _rms_scale(xp), 0.0)
    xn = xnext_ref[...]
    rn = jnp.where(has_next, _rms_scale(xn), 0.0)
    edge = 2 * SUBLANES
    row = lax.broadcasted_iota(jnp.int32, (edge, 1), 0)
    inv_first, inv_last = [], []
    for win in POOL_WINDOWS[:n_levels]:
        half = win // 2
        for first_pos, out in ((t0 + row, inv_first), (t0 + tm - edge + row, inv_last)):
            cnt = jnp.minimum(first_pos + half, seq) - jnp.maximum(first_pos - half, 0)
            out.append(1.0 / cnt.astype(F32))

    pad = POOL_HALO
    top = pad + POOL_HALO
    n_buf = n_ext + 2 * pad
    n_chunks = D_MODEL // LANES
    for ch in range(n_chunks):
        cols = slice(ch * LANES, (ch + 1) * LANES)
        gain = gain_ref[:, cols]
        hc = x_ref[:, cols] * r * gain
        hn_ref[:, cols] = hc.astype(BF16)
        h_ref[top:top + tm, cols] = hc
        h_ref[pad:top, cols] = xprev_ref[:, cols] * rp * gain
        h_ref[top + tm:top + tm + POOL_HALO, cols] = xnext_ref[:, cols] * rn * gain
    h_ref[0:pad, :] = jnp.zeros((pad, D_MODEL), F32)
    h_ref[n_buf - pad:, :] = jnp.zeros((pad, D_MODEL), F32)
    for k in range(n_groups):
        gcols = slice(k * c, (k + 1) * c)
        zs_ref[:, gcols] = _silu(_dot(hn_ref[...], wz_ref[:, gcols]))

    centre_blocks = [(top + b, top + b + POOL_ROW_BLOCK) for b in range(0, tm, POOL_ROW_BLOCK)]
    all_blocks = [(pad, top)] + centre_blocks + [(top + tm, top + tm + POOL_HALO)]
    wins = POOL_WINDOWS[first_group:first_group + n_groups]
    n_sets = lvl_ref.shape[0]
    for ch in range(n_chunks):
        cols = slice(ch * LANES, (ch + 1) * LANES)
        lvl = lvl_ref.at[ch % n_sets]
        for k, win in enumerate(wins):
            gi = first_group + k
            feeds_next = k + 1 < n_groups and wins[k + 1] > POOL_DIRECT_MAX
            for b0, b1 in (all_blocks if feeds_next else centre_blocks):
                if win <= POOL_DIRECT_MAX:
                    terms = [h_ref[b0 + u:b1 + u, cols] for u in range(-(win // 2), win // 2)]
                    while len(terms) > 1:
                        terms = [terms[j] + terms[j + 1] for j in range(0, len(terms), 2)]
                    level = terms[0]
                else:
                    assert win == 2 * wins[k - 1] and wins[k - 1] <= POOL_DIRECT_MAX
                    level = lvl[b0 - win // 4:b1 - win // 4, :] + lvl[b0 + win // 4:b1 + win // 4, :]
                if feeds_next:
                    lvl[b0:b1, :] = level
                r0, r1 = b0 - top, b1 - top
                if r0 < 0 or r1 > tm:
                    continue
                h = h_ref[b0:b1, cols]
                inv = 1.0 / win
                if r0 == 0:
                    a_ref[k, 0:edge, cols] = (level[:edge] * inv_first[gi] - h[:edge]).astype(BF16)
                    a_ref[k, edge:r1, cols] = (level[edge:] * inv - h[edge:]).astype(BF16)
                elif r1 == tm:
                    a_ref[k, r0:tm - edge, cols] = (level[:-edge] * inv - h[:-edge]).astype(BF16)
                    a_ref[k, tm - edge:tm, cols] = (level[-edge:] * inv_last[gi] - h[-edge:]
                                                    ).astype(BF16)
                else:
                    a_ref[k, r0:r1, cols] = (level * inv - h).astype(BF16)

    for k in range(n_groups):
        gcols = slice(k * c, (k + 1) * c)
        mm = _dot(a_ref[k], wug_ref[:, gcols])
        t_ref[:, gcols] = ((mm * scale_ref[:, gcols]) * zs_ref[:, gcols]).astype(BF16)


def _pool_front(x2d, gain, w_ug, w_z, scale, *, seq, part):
    rows = x2d.shape[0]
    tm = POOL_TM
    n_groups = POOL_GROUPS_PER_CALL
    width = n_groups * POOL_GROUP_WIDTH
    n_row_blocks8 = rows // SUBLANES
    resident = pl.Buffered(1)
    kernel = functools.partial(_pool_front_kernel, tm=tm, seq=seq, first_group=part * n_groups,
                               n_groups=n_groups)
    return pl.pallas_call(
        kernel,
        name="pool_front",
        grid=(rows // tm,),
        in_specs=[
            pl.BlockSpec((tm, D_MODEL), lambda i: (i, 0)),
            pl.BlockSpec((POOL_HALO, D_MODEL),
                         lambda i: (jnp.maximum(i * (tm // SUBLANES) - 1, 0), 0)),
            pl.BlockSpec((POOL_HALO, D_MODEL),
                         lambda i: (jnp.minimum((i + 1) * (tm // SUBLANES), n_row_blocks8 - 1), 0)),
            pl.BlockSpec((1, D_MODEL), lambda i: (0, 0)),
            pl.BlockSpec((D_MODEL, width), lambda i: (0, part), pipeline_mode=resident),
            pl.BlockSpec((D_MODEL, width), lambda i: (0, part), pipeline_mode=resident),
            pl.BlockSpec((1, width), lambda i: (0, part)),
        ],
        out_specs=pl.BlockSpec((tm, width), lambda i: (i, 0)),
        out_shape=jax.ShapeDtypeStruct((rows, width), BF16),
        scratch_shapes=[
            pltpu.VMEM((tm, D_MODEL), BF16),
            pltpu.VMEM((n_groups, tm, D_MODEL), BF16),
            pltpu.VMEM((tm, width), F32),
            pltpu.VMEM((tm + 4 * POOL_HALO, D_MODEL), F32),
            pltpu.VMEM((2, tm + 4 * POOL_HALO, LANES), F32),
        ],
        compiler_params=pltpu.CompilerParams(
            dimension_semantics=("arbitrary",),
            vmem_limit_bytes=VMEM_LIMIT_BYTES),
    )(x2d, x2d, x2d, gain, w_ug, w_z, scale)


def _out_pe_math(t_refs, wout_ref, x_ref, p_ref, wp_ref, gain_ref, wgate_ref, out_ref):
    y = None
    k0 = 0
    for t_ref in t_refs:
        k1 = k0 + t_ref.shape[1]
        part = _dot(t_ref[...], wout_ref[k0:k1, :])
        y = part if y is None else y + part
        k0 = k1
    h = x_ref[...] + y
    hn = (h * _rms_scale(h) * gain_ref[...]).astype(BF16)
    gate = jax.nn.sigmoid(_dot(hn, wgate_ref[...]))
    e = _dot(p_ref[...].astype(BF16), wp_ref[...])
    out_ref[...] = h + e * gate


def _out_pe_kernel(*refs, n_parts, group_tiles, per_group):
    n_groups = len(group_tiles)
    n_sources = n_groups if per_group else 1
    refs = list(refs)

    def take(count):
        taken = refs[:count]
        del refs[:count]
        return taken

    t_sets = [take(n_parts) for _ in range(n_sources)]
    x_refs = take(n_sources)
    p_refs = take(n_groups)
    wout_ref, wp_ref, gain_ref, wgate_ref, out_ref = take(5)
    assert not refs

    i = pl.program_id(0)
    start = 0
    for g, tiles in enumerate(group_tiles):
        src = g if per_group else 0
        body = functools.partial(_out_pe_math, t_sets[src], wout_ref, x_refs[src], p_refs[g], wp_ref,
                                 gain_ref, wgate_ref, out_ref)
        pl.when((i >= start) & (i < start + tiles))(body)
        start += tiles


def _out_pe_tile(k, n_t, n_x, n_p):
    weights = 2 * (k * D_MODEL + PLE_DIM * D_MODEL + D_MODEL * D_MODEL)
    for tm in OUT_TM_CHOICES:
        row_blocks = 2 * tm * (n_t * 2 * k + n_x * 4 * D_MODEL + n_p * 4 * PLE_DIM + 4 * D_MODEL)
        temporaries = 2 * tm * D_MODEL * 4
        if weights + row_blocks + temporaries <= VMEM_LIMIT_BYTES:
            return tm
    raise ValueError(f"no out_pe row tile fits VMEM for k={k}")


def _out_pe(t_sets, w_out, x_list, p_list, layer, w_proj, gain, w_gate, *, group_rows,
            combined_row_offset=None):
    n_groups = len(group_rows)
    n_parts = len(t_sets[0])
    k = sum(t.shape[1] for t in t_sets[0])
    assert k == w_out.shape[0] and len(p_list) == n_groups
    tm = _out_pe_tile(k, len(t_sets), len(x_list), n_groups)
    group_tiles = tuple(rows // tm for rows in group_rows)
    starts = tuple(sum(group_tiles[:g]) for g in range(n_groups))
    total_rows = sum(group_rows)
    resident = pl.Buffered(1)

    per_group = combined_row_offset is None
    assert len(t_sets) == len(x_list) == (n_groups if per_group else 1)
    offset_tiles = 0 if per_group else combined_row_offset // tm

    def combined(width, offset=offset_tiles):
        return pl.BlockSpec((tm, width), lambda i: (i + offset, 0))

    def grouped(width, g, block_offset=0):
        lo, n = starts[g], group_tiles[g]
        return pl.BlockSpec((tm, width), lambda i: (jnp.clip(i - lo, 0, n - 1) + block_offset, 0))

    in_specs, operands = [], []
    for g, parts in enumerate(t_sets):
        for t in parts:
            in_specs.append(grouped(t.shape[1], g) if per_group else combined(t.shape[1]))
            operands.append(t)
    for g, x in enumerate(x_list):
        in_specs.append(grouped(D_MODEL, g) if per_group else combined(D_MODEL))
        operands.append(x)
    for g, p in enumerate(p_list):
        in_specs.append(grouped(PLE_DIM, g, block_offset=layer * group_tiles[g]))
        operands.append(p)
    in_specs += [
        pl.BlockSpec((k, D_MODEL), lambda i: (0, 0), pipeline_mode=resident),
        pl.BlockSpec((PLE_DIM, D_MODEL), lambda i: (0, 0), pipeline_mode=resident),
        pl.BlockSpec((1, D_MODEL), lambda i: (0, 0)),
        pl.BlockSpec((D_MODEL, D_MODEL), lambda i: (0, 0), pipeline_mode=resident),
    ]
    operands += [w_out, w_proj, gain, w_gate]
    kernel = functools.partial(_out_pe_kernel, n_parts=n_parts, group_tiles=group_tiles,
                               per_group=per_group)
    return pl.pallas_call(
        kernel,
        name="out_pe",
        grid=(total_rows // tm,),
        in_specs=in_specs,
        out_specs=combined(D_MODEL, offset=0),
        out_shape=jax.ShapeDtypeStruct((total_rows, D_MODEL), F32),
        compiler_params=pltpu.CompilerParams(
            dimension_semantics=("arbitrary",),
            vmem_limit_bytes=VMEM_LIMIT_BYTES),
    )(*operands)


def _attn_front_kernel(x_ref, gain_ref, w_ref, cos_ref, sin_ref, qg_ref, kg_ref,
                       q_ref, k_ref, v_ref, zs_ref, hn_ref):
    x = x_ref[...]
    hn_ref[...] = (x * _rms_scale(x) * gain_ref[...]).astype(BF16)
    cos = cos_ref[...].T
    sin = sin_ref[...].T
    half = ROPE_DIM // 2
    first_half = lax.broadcasted_iota(jnp.int32, cos.shape, 1) < half

    def norm_rope(slab, head_gain):
        s = slab * _rms_scale(slab) * head_gain
        partner = jnp.where(first_half, pltpu.roll(s, HEAD_DIM - half, axis=1),
                            pltpu.roll(s, half, axis=1))
        return s * cos + partner * sin

    tn = ATTN_FRONT_TN
    heads_per_block = tn // HEAD_DIM
    for n in range(ATTN_IN_WIDTH // tn):
        y = _dot(hn_ref[...], w_ref[:, n * tn:(n + 1) * tn])
        if n < Q_WIDTH // tn:
            for h in range(heads_per_block):
                cols = slice(h * HEAD_DIM, (h + 1) * HEAD_DIM)
                q_ref[:, n * tn + h * HEAD_DIM:n * tn + (h + 1) * HEAD_DIM] = (
                    norm_rope(y[:, cols], qg_ref[...]).astype(BF16))
        elif n == Q_WIDTH // tn:
            for h in range(N_KV_HEADS):
                cols = slice(h * HEAD_DIM, (h + 1) * HEAD_DIM)
                k_ref[:, cols] = norm_rope(y[:, cols], kg_ref[...]).astype(BF16)
            v_ref[...] = y[:, KV_WIDTH:].astype(BF16)
        else:
            z0 = (n - Q_WIDTH // tn - 1) * tn
            zs_ref[:, z0:z0 + tn] = _silu(y)


def _attn_front(x2d, gain, w_in, cos_t, sin_t, q_gain, k_gain, *, group_rows, group_seqs):
    rows = x2d.shape[0]
    tm = ATTN_FRONT_TM
    assert 2 * KV_WIDTH == ATTN_FRONT_TN

    def table_map(i):
        t0, _ = _tile_position(i * tm, group_rows, group_seqs)
        return (0, t0 // tm)

    table_spec = pl.BlockSpec((HEAD_DIM, tm), table_map)
    row_spec = lambda width: pl.BlockSpec((tm, width), lambda i: (i, 0))
    return pl.pallas_call(
        _attn_front_kernel,
        name="attn_front",
        grid=(rows // tm,),
        in_specs=[
            row_spec(D_MODEL),
            pl.BlockSpec((1, D_MODEL), lambda i: (0, 0)),
            pl.BlockSpec((D_MODEL, ATTN_IN_WIDTH), lambda i: (0, 0), pipeline_mode=pl.Buffered(1)),
            table_spec, table_spec,
            pl.BlockSpec((1, HEAD_DIM), lambda i: (0, 0)),
            pl.BlockSpec((1, HEAD_DIM), lambda i: (0, 0)),
        ],
        out_specs=[row_spec(Q_WIDTH), row_spec(KV_WIDTH), row_spec(KV_WIDTH), row_spec(Q_WIDTH)],
        out_shape=[
            jax.ShapeDtypeStruct((rows, Q_WIDTH), BF16),
            jax.ShapeDtypeStruct((rows, KV_WIDTH), BF16),
            jax.ShapeDtypeStruct((rows, KV_WIDTH), BF16),
            jax.ShapeDtypeStruct((rows, Q_WIDTH), F32),
        ],
        scratch_shapes=[pltpu.VMEM((tm, D_MODEL), BF16)],
        compiler_params=pltpu.CompilerParams(
            dimension_semantics=("arbitrary",),
            vmem_limit_bytes=VMEM_LIMIT_BYTES),
    )(x2d, gain, w_in, cos_t, sin_t, q_gain, k_gain)


def _attn_core_kernel(sink_ref, q_ref, kprev_ref, kmid_ref, knext_ref, vprev_ref, vmid_ref,
                      vnext_ref, zs_ref, o_ref, kbuf_ref, vbuf_ref, bias_ref, *, tm, group_rows,
                      group_seqs):
    i = pl.program_id(0)
    blk = ATTN_BLOCK
    n_qb = tm // blk
    t0, seq = _tile_position(i * tm, group_rows, group_seqs)
    kbuf_ref[0:blk, :] = kprev_ref[...]
    kbuf_ref[blk:blk + tm, :] = kmid_ref[...]
    kbuf_ref[blk + tm:, :] = knext_ref[...]
    for h in range(N_KV_HEADS):
        src = slice(h * HEAD_DIM, (h + 1) * HEAD_DIM)
        dst = slice(2 * h * HEAD_DIM, (2 * h + 1) * HEAD_DIM)
        vbuf_ref[0:blk, dst] = vprev_ref[:, src]
        vbuf_ref[blk:blk + tm, dst] = vmid_ref[:, src]
        vbuf_ref[blk + tm:, dst] = vnext_ref[:, src]
        vbuf_ref[:, (2 * h + 1) * HEAD_DIM:(2 * h + 2) * HEAD_DIM] = jnp.ones(
            (tm + 2 * blk, HEAD_DIM), BF16)

    n_rows = GQA_GROUP * blk
    qi = lax.rem(lax.broadcasted_iota(jnp.int32, (n_rows, blk), 0), blk)
    kj = lax.broadcasted_iota(jnp.int32, (n_rows, blk), 1)
    prev_bias = jnp.where(kj >= qi, 0.0, NEG_BIG).astype(F32)
    next_bias = jnp.where(kj <= qi, 0.0, NEG_BIG).astype(F32)
    bias_ref[0] = jnp.where(t0 == 0, NEG_BIG, prev_bias)
    bias_ref[1] = prev_bias
    bias_ref[2] = jnp.where(t0 + tm == seq, NEG_BIG, next_bias)
    bias_ref[3] = next_bias

    sqrt_d = math.sqrt(HEAD_DIM)
    exp2_scale = LOG2_E / sqrt_d

    for qb in range(n_qb):
        row0 = qb * blk
        rows = pl.ds(row0, blk)
        b_prev = bias_ref[0 if qb == 0 else 1]
        b_next = bias_ref[2 if qb == n_qb - 1 else 3]
        for kvh in range(N_KV_HEADS):
            heads = [kvh * GQA_GROUP + hg for hg in range(GQA_GROUP)]
            qs = jnp.concatenate(
                [q_ref[rows, h * HEAD_DIM:(h + 1) * HEAD_DIM] for h in heads], axis=0)
            kw = kbuf_ref[pl.ds(row0, 3 * blk), kvh * HEAD_DIM:(kvh + 1) * HEAD_DIM]
            vw = vbuf_ref[pl.ds(row0, 3 * blk), 2 * kvh * HEAD_DIM:(2 * kvh + 2) * HEAD_DIM]
            s = lax.dot_general(qs, kw, (((1,), (1,)), ((), ())), preferred_element_type=F32)
            sb = (s[:, 0:blk] + b_prev, s[:, blk:2 * blk], s[:, 2 * blk:] + b_next)
            row_max = jnp.max(jnp.maximum(jnp.maximum(sb[0], sb[1]), sb[2]), axis=-1, keepdims=True)
            e_rows, m_rows, sinks = [], [], []
            for hg, h in enumerate(heads):
                r = slice(hg * blk, (hg + 1) * blk)
                sink_raw = sink_ref[h] * sqrt_d
                m_h = jnp.maximum(row_max[r], sink_raw)
                e_rows.append(jnp.concatenate(
                    [jnp.exp2((b[r] - m_h) * exp2_scale).astype(BF16) for b in sb], axis=1))
                m_rows.append(m_h)
                sinks.append(sink_raw)
            e = jnp.concatenate(e_rows, axis=0)
            ov = _dot(e, vw)
            for hg, h in enumerate(heads):
                r = slice(hg * blk, (hg + 1) * blk)
                hcols = slice(h * HEAD_DIM, (h + 1) * HEAD_DIM)
                denom = ov[r, HEAD_DIM:] + jnp.exp2((sinks[hg] - m_rows[hg]) * exp2_scale)
                o_ref[rows, hcols] = (ov[r, :HEAD_DIM] * (1.0 / denom) * zs_ref[rows, hcols]
                                      ).astype(BF16)


def _attn_core(sink, q, k, v, zs, *, group_rows, group_seqs):
    rows = q.shape[0]
    tm = ATTN_CORE_TM
    blk = ATTN_BLOCK
    per = tm // blk
    n_blocks = rows // blk
    prev_map = lambda i: (jnp.maximum(i * per - 1, 0), 0)
    next_map = lambda i: (jnp.minimum((i + 1) * per, n_blocks - 1), 0)
    mid_map = lambda i: (i, 0)
    kernel = functools.partial(_attn_core_kernel, tm=tm, group_rows=group_rows,
                               group_seqs=group_seqs)
    return pl.pallas_call(
        kernel,
        name="attn_core",
        grid=(rows // tm,),
        in_specs=[
            pl.BlockSpec(memory_space=pltpu.SMEM),
            pl.BlockSpec((tm, Q_WIDTH), mid_map),
            pl.BlockSpec((blk, KV_WIDTH), prev_map),
            pl.BlockSpec((tm, KV_WIDTH), mid_map),
            pl.BlockSpec((blk, KV_WIDTH), next_map),
            pl.BlockSpec((blk, KV_WIDTH), prev_map),
            pl.BlockSpec((tm, KV_WIDTH), mid_map),
            pl.BlockSpec((blk, KV_WIDTH), next_map),
            pl.BlockSpec((tm, Q_WIDTH), mid_map),
        ],
        out_specs=pl.BlockSpec((tm, Q_WIDTH), mid_map),
        out_shape=jax.ShapeDtypeStruct((rows, Q_WIDTH), BF16),
        scratch_shapes=[
            pltpu.VMEM((tm + 2 * blk, KV_WIDTH), BF16),
            pltpu.VMEM((tm + 2 * blk, 2 * KV_WIDTH), BF16),
            pltpu.VMEM((4, GQA_GROUP * blk, blk), F32),
        ],
        compiler_params=pltpu.CompilerParams(
            dimension_semantics=("arbitrary",),
            vmem_limit_bytes=VMEM_LIMIT_BYTES),
    )(sink, q, k, k, k, v, v, v, zs)


def _rope_tables(seq):
    freq = ROPE_THETA ** (-jnp.arange(0, ROPE_DIM, 2, dtype=F32) / ROPE_DIM)
    pos = jnp.arange(seq, dtype=F32)
    ang = freq[:, None] * pos[None, :]
    cos = jnp.cos(ang)
    sin = jnp.sin(ang)
    rest = HEAD_DIM - ROPE_DIM
    cos_t = jnp.concatenate([cos, cos, jnp.ones((rest, seq), F32)], axis=0)
    sin_t = jnp.concatenate([-sin, sin, jnp.zeros((rest, seq), F32)], axis=0)
    return cos_t, sin_t


def kernel(x_prompt, x_sample, p_prompt, p_sample, norm_g, a_w_in, a_w_grp, a_scale, a_w_out,
           b_w_in, b_q_norm, b_k_norm, b_sink, b_w_out, pe_w_proj, pe_norm_g, pe_w_gate):
    xs = (x_prompt, x_sample)
    group_seqs = tuple(x.shape[1] for x in xs)
    group_rows = tuple(x.shape[0] * x.shape[1] for x in xs)
    x2d = [x.reshape(rows, D_MODEL) for x, rows in zip(xs, group_rows)]
    p2d = [p.reshape(p.shape[0] * rows, PLE_DIM) for p, rows in zip((p_prompt, p_sample), group_rows)]

    a_w_ug = _fold_group_proj(a_w_in[0], a_w_grp[0])
    a_w_z = a_w_in[0][:, POOL_WIDTH:].astype(BF16)
    a_w_out_bf = a_w_out[0].astype(BF16)
    b_w_in_bf = b_w_in[0].astype(BF16)
    b_w_out_bf = b_w_out[0].astype(BF16)
    pe_w_proj_bf = pe_w_proj.astype(BF16)
    pe_w_gate_bf = pe_w_gate.astype(BF16)
    rope_cos, rope_sin = _rope_tables(max(group_seqs))

    t_sets = [[_pool_front(x, norm_g[0:1], a_w_ug, a_w_z, a_scale, seq=seq, part=part)
               for part in range(N_POOL_GROUPS // POOL_GROUPS_PER_CALL)]
              for x, seq in zip(x2d, group_seqs)]
    x1 = _out_pe(t_sets, a_w_out_bf, x2d, p2d, 0, pe_w_proj_bf[0], pe_norm_g[0:1], pe_w_gate_bf[0],
                 group_rows=group_rows)

    q, k, v, zs = _attn_front(x1, norm_g[1:2], b_w_in_bf, rope_cos, rope_sin, b_q_norm, b_k_norm,
                              group_rows=group_rows, group_seqs=group_seqs)
    og = _attn_core(b_sink[0], q, k, v, zs, group_rows=group_rows, group_seqs=group_seqs)
    ys = [_out_pe([[og]], b_w_out_bf, [x1], [p], 1, pe_w_proj_bf[1], pe_norm_g[1:2], pe_w_gate_bf[1],
                  group_rows=(rows,), combined_row_offset=sum(group_rows[:g]))
          for g, (p, rows) in enumerate(zip(p2d, group_rows))]
    return tuple(y.reshape(x.shape) for y, x in zip(ys, xs))
```

```python
import functools
import math

import jax
import jax.numpy as jnp
from jax import lax
from jax.experimental import pallas as pl
from jax.experimental.pallas import tpu as pltpu

F32 = jnp.float32
BF16 = jnp.bfloat16

D_MODEL = 2048
PLE_DIM = 256
EPS = 1e-6

POOL_WIDTH = 4096
POOL_WINDOWS = (2, 4, 8, 16)
N_POOL_GROUPS = len(POOL_WINDOWS)
POOL_GROUP_WIDTH = POOL_WIDTH // N_POOL_GROUPS
POOL_HALO = 8

HEAD_DIM = 128
N_Q_HEADS = 16
N_KV_HEADS = 4
GQA_GROUP = N_Q_HEADS // N_KV_HEADS
Q_WIDTH = N_Q_HEADS * HEAD_DIM
KV_WIDTH = N_KV_HEADS * HEAD_DIM
ATTN_IN_WIDTH = 2 * Q_WIDTH + 2 * KV_WIDTH
ATTN_BLOCK = 128
ROPE_DIM = HEAD_DIM // 4
ROPE_THETA = 500000.0
NEG_BIG = -1e30
LOG2_E = math.log2(math.e)

VMEM_LIMIT_BYTES = 56 * 1024 * 1024
LANES = 128
SUBLANES = 8

FOLD_ROW_BLOCK = 1024
POOL_TM = 512
POOL_GROUPS_PER_CALL = 2
POOL_ROW_BLOCK = 64
POOL_DIRECT_MAX = 8
ATTN_FRONT_TM = 512
ATTN_FRONT_TN = 1024
ATTN_FRONT_SUB = 256
ATTN_CORE_TM = 512
OUT_TM_CHOICES = (512, 256)


def _rms_scale(x):
    return lax.rsqrt(jnp.mean(x * x, axis=-1, keepdims=True) + EPS)


def _dot(a, b):
    return jnp.dot(a, b, preferred_element_type=F32)


def _silu(z):
    return z * jax.nn.sigmoid(z)


def _tile_position(row0, group_rows, group_seqs):
    start = sum(group_rows[:-1])
    t0 = lax.rem(row0 - start, group_seqs[-1])
    seq = group_seqs[-1]
    for rows, s in zip(group_rows[-2::-1], group_seqs[-2::-1]):
        start -= rows
        inside = row0 < start + rows
        t0 = jnp.where(inside, lax.rem(row0 - start, s), t0)
        seq = jnp.where(inside, s, seq)
    return t0, seq


def _fold_group_kernel(wu_ref, wgrp_ref, wz_ref, wug_out_ref, wz_out_ref):
    wug_out_ref[...] = _dot(wu_ref[...].astype(BF16), wgrp_ref[0].astype(BF16)).astype(BF16)
    wz_out_ref[...] = wz_ref[...].astype(BF16)


def _fold_group_proj(w_in, w_grp):
    c = POOL_GROUP_WIDTH
    rows = FOLD_ROW_BLOCK
    out = jax.ShapeDtypeStruct((D_MODEL, POOL_WIDTH), BF16)
    return pl.pallas_call(
        _fold_group_kernel,
        name="fold_group_proj",
        grid=(N_POOL_GROUPS, D_MODEL // rows),
        in_specs=[
            pl.BlockSpec((rows, c), lambda g, r: (r, g)),
            pl.BlockSpec((1, c, c), lambda g, r: (g, 0, 0)),
            pl.BlockSpec((rows, c), lambda g, r: (r, N_POOL_GROUPS + g)),
        ],
        out_specs=[pl.BlockSpec((rows, c), lambda g, r: (r, g))] * 2,
        out_shape=[out, out],
        compiler_params=pltpu.CompilerParams(
            dimension_semantics=("arbitrary", "arbitrary"),
            vmem_limit_bytes=VMEM_LIMIT_BYTES),
    )(w_in, w_grp, w_in)


def _pool_front_kernel(x_ref, xprev_ref, xnext_ref, gain_ref, wug_ref, wz_ref,
                       scale_ref, t_ref, hn_ref, a_ref, zs_ref, h_ref, lvl_ref, *, tm, seq,
                       first_group, n_groups):
    i = pl.program_id(0)
    n_ext = tm + 2 * POOL_HALO
    c = POOL_GROUP_WIDTH
    n_levels = first_group + n_groups
    t0 = lax.rem(i * tm, seq)
    has_prev = t0 > 0
    has_next = t0 + tm < seq
    x = x_ref[...]
    r = _rms_scale(x)
    xp = xprev_ref[...]
    rp = jnp.where(has_prev, _rms_scale(xp), 0.0)
    xn = xnext_ref[...]
    rn = jnp.where(has_next, _rms_scale(xn), 0.0)
    edge = 2 * SUBLANES
    row = lax.broadcasted_iota(jnp.int32, (edge, 1), 0)
    inv_first, inv_last = [], []
    for win in POOL_WINDOWS[:n_levels]:
        half = win // 2
        for first_pos, out in ((t0 + row, inv_first), (t0 + tm - edge + row, inv_last)):
            cnt = jnp.minimum(first_pos + half, seq) - jnp.maximum(first_pos - half, 0)
            out.append(1.0 / cnt.astype(F32))

    pad = POOL_HALO
    top = pad + POOL_HALO
    n_buf = n_ext + 2 * pad
    n_chunks = D_MODEL // LANES
    for ch in range(n_chunks):
        cols = slice(ch * LANES, (ch + 1) * LANES)
        gain = gain_ref[:, cols]
        hc = x_ref[:, cols] * r * gain
        hn_ref[:, cols] = hc.astype(BF16)
        h_ref[top:top + tm, cols] = hc
        h_ref[pad:top, cols] = xprev_ref[:, cols] * rp * gain
        h_ref[top + tm:top + tm + POOL_HALO, cols] = xnext_ref[:, cols] * rn * gain
    h_ref[0:pad, :] = jnp.zeros((pad, D_MODEL), F32)
    h_ref[n_buf - pad:, :] = jnp.zeros((pad, D_MODEL), F32)
    for k in range(n_groups):
        gcols = slice(k * c, (k + 1) * c)
        zs_ref[:, gcols] = _silu(_dot(hn_ref[...], wz_ref[:, gcols]))

    centre_blocks = [(top + b, top + b + POOL_ROW_BLOCK) for b in range(0, tm, POOL_ROW_BLOCK)]
    all_blocks = [(pad, top)] + centre_blocks + [(top + tm, top + tm + POOL_HALO)]
    wins = POOL_WINDOWS[first_group:first_group + n_groups]
    n_sets = lvl_ref.shape[0]
    for ch in range(n_chunks):
        cols = slice(ch * LANES, (ch + 1) * LANES)
        lvl = lvl_ref.at[ch % n_sets]
        for k, win in enumerate(wins):
            gi = first_group + k
            feeds_next = k + 1 < n_groups and wins[k + 1] > POOL_DIRECT_MAX
            for b0, b1 in (all_blocks if feeds_next else centre_blocks):
                if win <= POOL_DIRECT_MAX:
                    terms = [h_ref[b0 + u:b1 + u, cols] for u in range(-(win // 2), win // 2)]
                    while len(terms) > 1:
                        terms = [terms[j] + terms[j + 1] for j in range(0, len(terms), 2)]
                    level = terms[0]
                else:
                    assert win == 2 * wins[k - 1] and wins[k - 1] <= POOL_DIRECT_MAX
                    level = lvl[b0 - win // 4:b1 - win // 4, :] + lvl[b0 + win // 4:b1 + win // 4, :]
                if feeds_next:
                    lvl[b0:b1, :] = level
                r0, r1 = b0 - top, b1 - top
                if r0 < 0 or r1 > tm:
                    continue
                h = h_ref[b0:b1, cols]
                inv = 1.0 / win
                if r0 == 0:
                    a_ref[k, 0:edge, cols] = (level[:edge] * inv_first[gi] - h[:edge]).astype(BF16)
                    a_ref[k, edge:r1, cols] = (level[edge:] * inv - h[edge:]).astype(BF16)
                elif r1 == tm:
                    a_ref[k, r0:tm - edge, cols] = (level[:-edge] * inv - h[:-edge]).astype(BF16)
                    a_ref[k, tm - edge:tm, cols] = (level[-edge:] * inv_last[gi] - h[-edge:]
                                                    ).astype(BF16)
                else:
                    a_ref[k, r0:r1, cols] = (level * inv - h).astype(BF16)

    for k in range(n_groups):
        gcols = slice(k * c, (k + 1) * c)
        mm = _dot(a_ref[k], wug_ref[:, gcols])
        t_ref[:, gcols] = ((mm * scale_ref[:, gcols]) * zs_ref[:, gcols]).astype(BF16)


def _pool_front(x2d, gain, w_ug, w_z, scale, *, seq, part):
    rows = x2d.shape[0]
    tm = POOL_TM
    n_groups = POOL_GROUPS_PER_CALL
    width = n_groups * POOL_GROUP_WIDTH
    n_row_blocks8 = rows // SUBLANES
    resident = pl.Buffered(1)
    kernel = functools.partial(_pool_front_kernel, tm=tm, seq=seq, first_group=part * n_groups,
                               n_groups=n_groups)
    return pl.pallas_call(
        kernel,
        name="pool_front",
        grid=(rows // tm,),
        in_specs=[
            pl.BlockSpec((tm, D_MODEL), lambda i: (i, 0)),
            pl.BlockSpec((POOL_HALO, D_MODEL),
                         lambda i: (jnp.maximum(i * (tm // SUBLANES) - 1, 0), 0)),
            pl.BlockSpec((POOL_HALO, D_MODEL),
                         lambda i: (jnp.minimum((i + 1) * (tm // SUBLANES), n_row_blocks8 - 1), 0)),
            pl.BlockSpec((1, D_MODEL), lambda i: (0, 0)),
            pl.BlockSpec((D_MODEL, width), lambda i: (0, part), pipeline_mode=resident),
            pl.BlockSpec((D_MODEL, width), lambda i: (0, part), pipeline_mode=resident),
            pl.BlockSpec((1, width), lambda i: (0, part)),
        ],
        out_specs=pl.BlockSpec((tm, width), lambda i: (i, 0)),
        out_shape=jax.ShapeDtypeStruct((rows, width), BF16),
        scratch_shapes=[
            pltpu.VMEM((tm, D_MODEL), BF16),
            pltpu.VMEM((n_groups, tm, D_MODEL), BF16),
            pltpu.VMEM((tm, width), F32),
            pltpu.VMEM((tm + 4 * POOL_HALO, D_MODEL), F32),
            pltpu.VMEM((2, tm + 4 * POOL_HALO, LANES), F32),
        ],
        compiler_params=pltpu.CompilerParams(
            dimension_semantics=("arbitrary",),
            vmem_limit_bytes=VMEM_LIMIT_BYTES),
    )(x2d, x2d, x2d, gain, w_ug, w_z, scale)


def _out_pe_math(t_refs, wout_ref, x_ref, p_ref, wp_ref, gain_ref, wgate_ref, out_ref):
    y = None
    k0 = 0
    for t_ref in t_refs:
        k1 = k0 + t_ref.shape[1]
        part = _dot(t_ref[...], wout_ref[k0:k1, :])
        y = part if y is None else y + part
        k0 = k1
    h = x_ref[...] + y
    hn = (h * _rms_scale(h) * gain_ref[...]).astype(BF16)
    gate = jax.nn.sigmoid(_dot(hn, wgate_ref[...]))
    e = _dot(p_ref[...].astype(BF16), wp_ref[...])
    out_ref[...] = h + e * gate


def _out_pe_kernel(*refs, n_parts, group_tiles, per_group):
    n_groups = len(group_tiles)
    n_sources = n_groups if per_group else 1
    refs = list(refs)

    def take(count):
        taken = refs[:count]
        del refs[:count]
        return taken

    t_sets = [take(n_parts) for _ in range(n_sources)]
    x_refs = take(n_sources)
    p_refs = take(n_groups)
    wout_ref, wp_ref, gain_ref, wgate_ref, out_ref = take(5)
    assert not refs

    i = pl.program_id(0)
    start = 0
    for g, tiles in enumerate(group_tiles):
        src = g if per_group else 0
        body = functools.partial(_out_pe_math, t_sets[src], wout_ref, x_refs[src], p_refs[g], wp_ref,
                                 gain_ref, wgate_ref, out_ref)
        pl.when((i >= start) & (i < start + tiles))(body)
        start += tiles


def _out_pe_tile(k, n_t, n_x, n_p):
    weights = 2 * (k * D_MODEL + PLE_DIM * D_MODEL + D_MODEL * D_MODEL)
    for tm in OUT_TM_CHOICES:
        row_blocks = 2 * tm * (n_t * 2 * k + n_x * 4 * D_MODEL + n_p * 4 * PLE_DIM + 4 * D_MODEL)
        temporaries = 2 * tm * D_MODEL * 4
        if weights + row_blocks + temporaries <= VMEM_LIMIT_BYTES:
            return tm
    raise ValueError(f"no out_pe row tile fits VMEM for k={k}")


def _out_pe(t_sets, w_out, x_list, p_list, layer, w_proj, gain, w_gate, *, group_rows,
            combined_row_offset=None):
    n_groups = len(group_rows)
    n_parts = len(t_sets[0])
    k = sum(t.shape[1] for t in t_sets[0])
    assert k == w_out.shape[0] and len(p_list) == n_groups
    tm = _out_pe_tile(k, len(t_sets), len(x_list), n_groups)
    group_tiles = tuple(rows // tm for rows in group_rows)
    starts = tuple(sum(group_tiles[:g]) for g in range(n_groups))
    total_rows = sum(group_rows)
    resident = pl.Buffered(1)

    per_group = combined_row_offset is None
    assert len(t_sets) == len(x_list) == (n_groups if per_group else 1)
    offset_tiles = 0 if per_group else combined_row_offset // tm

    def combined(width, offset=offset_tiles):
        return pl.BlockSpec((tm, width), lambda i: (i + offset, 0))

    def grouped(width, g, block_offset=0):
        lo, n = starts[g], group_tiles[g]
        return pl.BlockSpec((tm, width), lambda i: (jnp.clip(i - lo, 0, n - 1) + block_offset, 0))

    in_specs, operands = [], []
    for g, parts in enumerate(t_sets):
        for t in parts:
            in_specs.append(grouped(t.shape[1], g) if per_group else combined(t.shape[1]))
            operands.append(t)
    for g, x in enumerate(x_list):
        in_specs.append(grouped(D_MODEL, g) if per_group else combined(D_MODEL))
        operands.append(x)
    for g, p in enumerate(p_list):
        in_specs.append(grouped(PLE_DIM, g, block_offset=layer * group_tiles[g]))
        operands.append(p)
    in_specs += [
        pl.BlockSpec((k, D_MODEL), lambda i: (0, 0), pipeline_mode=resident),
        pl.BlockSpec((PLE_DIM, D_MODEL), lambda i: (0, 0), pipeline_mode=resident),
        pl.BlockSpec((1, D_MODEL), lambda i: (0, 0)),
        pl.BlockSpec((D_MODEL, D_MODEL), lambda i: (0, 0), pipeline_mode=resident),
    ]
    operands += [w_out, w_proj, gain, w_gate]
    kernel = functools.partial(_out_pe_kernel, n_parts=n_parts, group_tiles=group_tiles,
                               per_group=per_group)
    return pl.pallas_call(
        kernel,
        name="out_pe",
        grid=(total_rows // tm,),
        in_specs=in_specs,
        out_specs=combined(D_MODEL, offset=0),
        out_shape=jax.ShapeDtypeStruct((total_rows, D_MODEL), F32),
        compiler_params=pltpu.CompilerParams(
            dimension_semantics=("arbitrary",),
            vmem_limit_bytes=VMEM_LIMIT_BYTES),
    )(*operands)


def _attn_front_kernel(x_ref, gain_ref, w_ref, cos_ref, sin_ref, qg_ref, kg_ref,
                       q_ref, k_ref, v_ref, zs_ref, hn_ref):
    tm = x_ref.shape[0]
    half = ROPE_DIM // 2
    tn = ATTN_FRONT_TN
    heads_per_block = tn // HEAD_DIM

    for r0 in range(0, tm, ATTN_FRONT_SUB):
        rows = slice(r0, r0 + ATTN_FRONT_SUB)
        x = x_ref[rows, :]
        hn_ref[rows, :] = (x * _rms_scale(x) * gain_ref[...]).astype(BF16)
        cos = cos_ref[:, rows].T
        sin = sin_ref[:, rows].T
        first_half = lax.broadcasted_iota(jnp.int32, cos.shape, 1) < half

        def norm_rope(slab, head_gain, cos=cos, sin=sin, first_half=first_half):
            s = slab * _rms_scale(slab) * head_gain
            partner = jnp.where(first_half, pltpu.roll(s, HEAD_DIM - half, axis=1),
                                pltpu.roll(s, half, axis=1))
            return s * cos + partner * sin

        for n in range(ATTN_IN_WIDTH // tn):
            y = _dot(hn_ref[rows, :], w_ref[:, n * tn:(n + 1) * tn])
            if n < Q_WIDTH // tn:
                for h in range(heads_per_block):
                    cols = slice(h * HEAD_DIM, (h + 1) * HEAD_DIM)
                    q_ref[rows, n * tn + h * HEAD_DIM:n * tn + (h + 1) * HEAD_DIM] = (
                        norm_rope(y[:, cols], qg_ref[...]).astype(BF16))
            elif n == Q_WIDTH // tn:
                for h in range(N_KV_HEADS):
                    cols = slice(h * HEAD_DIM, (h + 1) * HEAD_DIM)
                    k_ref[rows, cols] = norm_rope(y[:, cols], kg_ref[...]).astype(BF16)
                v_ref[rows, :] = y[:, KV_WIDTH:].astype(BF16)
            else:
                z0 = (n - Q_WIDTH // tn - 1) * tn
                zs_ref[rows, z0:z0 + tn] = _silu(y)


def _attn_front(x2d, gain, w_in, cos_t, sin_t, q_gain, k_gain, *, group_rows, group_seqs):
    rows = x2d.shape[0]
    tm = ATTN_FRONT_TM
    assert 2 * KV_WIDTH == ATTN_FRONT_TN

    def table_map(i):
        t0, _ = _tile_position(i * tm, group_rows, group_seqs)
        return (0, t0 // tm)

    table_spec = pl.BlockSpec((HEAD_DIM, tm), table_map)
    row_spec = lambda width: pl.BlockSpec((tm, width), lambda i: (i, 0))
    return pl.pallas_call(
        _attn_front_kernel,
        name="attn_front",
        grid=(rows // tm,),
        in_specs=[
            row_spec(D_MODEL),
            pl.BlockSpec((1, D_MODEL), lambda i: (0, 0)),
            pl.BlockSpec((D_MODEL, ATTN_IN_WIDTH), lambda i: (0, 0), pipeline_mode=pl.Buffered(1)),
            table_spec, table_spec,
            pl.BlockSpec((1, HEAD_DIM), lambda i: (0, 0)),
            pl.BlockSpec((1, HEAD_DIM), lambda i: (0, 0)),
        ],
        out_specs=[row_spec(Q_WIDTH), row_spec(KV_WIDTH), row_spec(KV_WIDTH), row_spec(Q_WIDTH)],
        out_shape=[
            jax.ShapeDtypeStruct((rows, Q_WIDTH), BF16),
            jax.ShapeDtypeStruct((rows, KV_WIDTH), BF16),
            jax.ShapeDtypeStruct((rows, KV_WIDTH), BF16),
            jax.ShapeDtypeStruct((rows, Q_WIDTH), F32),
        ],
        scratch_shapes=[pltpu.VMEM((tm, D_MODEL), BF16)],
        compiler_params=pltpu.CompilerParams(
            dimension_semantics=("arbitrary",),
            vmem_limit_bytes=VMEM_LIMIT_BYTES),
    )(x2d, gain, w_in, cos_t, sin_t, q_gain, k_gain)


def _attn_core_kernel(sink_ref, q_ref, kprev_ref, kmid_ref, knext_ref, vprev_ref, vmid_ref,
                      vnext_ref, zs_ref, o_ref, kbuf_ref, vbuf_ref, bias_ref, *, tm, group_rows,
                      group_seqs):
    i = pl.program_id(0)
    blk = ATTN_BLOCK
    n_qb = tm // blk
    t0, seq = _tile_position(i * tm, group_rows, group_seqs)
    kbuf_ref[0:blk, :] = kprev_ref[...]
    kbuf_ref[blk:blk + tm, :] = kmid_ref[...]
    kbuf_ref[blk + tm:, :] = knext_ref[...]
    for h in range(N_KV_HEADS):
        src = slice(h * HEAD_DIM, (h + 1) * HEAD_DIM)
        dst = slice(2 * h * HEAD_DIM, (2 * h + 1) * HEAD_DIM)
        vbuf_ref[0:blk, dst] = vprev_ref[:, src]
        vbuf_ref[blk:blk + tm, dst] = vmid_ref[:, src]
        vbuf_ref[blk + tm:, dst] = vnext_ref[:, src]
        vbuf_ref[:, (2 * h + 1) * HEAD_DIM:(2 * h + 2) * HEAD_DIM] = jnp.ones(
            (tm + 2 * blk, HEAD_DIM), BF16)

    n_rows = GQA_GROUP * blk
    qi = lax.rem(lax.broadcasted_iota(jnp.int32, (n_rows, blk), 0), blk)
    kj = lax.broadcasted_iota(jnp.int32, (n_rows, blk), 1)
    prev_bias = jnp.where(kj >= qi, 0.0, NEG_BIG).astype(F32)
    next_bias = jnp.where(kj <= qi, 0.0, NEG_BIG).astype(F32)
    bias_ref[0] = jnp.where(t0 == 0, NEG_BIG, prev_bias)
    bias_ref[1] = prev_bias
    bias_ref[2] = jnp.where(t0 + tm == seq, NEG_BIG, next_bias)
    bias_ref[3] = next_bias

    sqrt_d = math.sqrt(HEAD_DIM)
    exp2_scale = LOG2_E / sqrt_d

    for qb in range(n_qb):
        row0 = qb * blk
        rows = pl.ds(row0, blk)
        b_prev = bias_ref[0 if qb == 0 else 1]
        b_next = bias_ref[2 if qb == n_qb - 1 else 3]
        for kvh in range(N_KV_HEADS):
            heads = [kvh * GQA_GROUP + hg for hg in range(GQA_GROUP)]
            qs = jnp.concatenate(
                [q_ref[rows, h * HEAD_DIM:(h + 1) * HEAD_DIM] for h in heads], axis=0)
            kw = kbuf_ref[pl.ds(row0, 3 * blk), kvh * HEAD_DIM:(kvh + 1) * HEAD_DIM]
            vw = vbuf_ref[pl.ds(row0, 3 * blk), 2 * kvh * HEAD_DIM:(2 * kvh + 2) * HEAD_DIM]
            s = lax.dot_general(qs, kw, (((1,), (1,)), ((), ())), preferred_element_type=F32)
            sb = (s[:, 0:blk] + b_prev, s[:, blk:2 * blk], s[:, 2 * blk:] + b_next)
            row_max = jnp.max(jnp.maximum(jnp.maximum(sb[0], sb[1]), sb[2]), axis=-1, keepdims=True)
            e_rows, m_rows, sinks = [], [], []
            for hg, h in enumerate(heads):
                r = slice(hg * blk, (hg + 1) * blk)
                sink_raw = sink_ref[h] * sqrt_d
                m_h = jnp.maximum(row_max[r], sink_raw)
                e_rows.append(jnp.concatenate(
                    [jnp.exp2((b[r] - m_h) * exp2_scale).astype(BF16) for b in sb], axis=1))
                m_rows.append(m_h)
                sinks.append(sink_raw)
            e = jnp.concatenate(e_rows, axis=0)
            ov = _dot(e, vw)
            for hg, h in enumerate(heads):
                r = slice(hg * blk, (hg + 1) * blk)
                hcols = slice(h * HEAD_DIM, (h + 1) * HEAD_DIM)
                denom = ov[r, HEAD_DIM:] + jnp.exp2((sinks[hg] - m_rows[hg]) * exp2_scale)
                o_ref[rows, hcols] = (ov[r, :HEAD_DIM] * (1.0 / denom) * zs_ref[rows, hcols]
                                      ).astype(BF16)


def _attn_core(sink, q, k, v, zs, *, group_rows, group_seqs):
    rows = q.shape[0]
    tm = ATTN_CORE_TM
    blk = ATTN_BLOCK
    per = tm // blk
    n_blocks = rows // blk
    prev_map = lambda i: (jnp.maximum(i * per - 1, 0), 0)
    next_map = lambda i: (jnp.minimum((i + 1) * per, n_blocks - 1), 0)
    mid_map = lambda i: (i, 0)
    kernel = functools.partial(_attn_core_kernel, tm=tm, group_rows=group_rows,
                               group_seqs=group_seqs)
    return pl.pallas_call(
        kernel,
        name="attn_core",
        grid=(rows // tm,),
        in_specs=[
            pl.BlockSpec(memory_space=pltpu.SMEM),
            pl.BlockSpec((tm, Q_WIDTH), mid_map),
            pl.BlockSpec((blk, KV_WIDTH), prev_map),
            pl.BlockSpec((tm, KV_WIDTH), mid_map),
            pl.BlockSpec((blk, KV_WIDTH), next_map),
            pl.BlockSpec((blk, KV_WIDTH), prev_map),
            pl.BlockSpec((tm, KV_WIDTH), mid_map),
            pl.BlockSpec((blk, KV_WIDTH), next_map),
            pl.BlockSpec((tm, Q_WIDTH), mid_map),
        ],
        out_specs=pl.BlockSpec((tm, Q_WIDTH), mid_map),
        out_shape=jax.ShapeDtypeStruct((rows, Q_WIDTH), BF16),
        scratch_shapes=[
            pltpu.VMEM((tm + 2 * blk, KV_WIDTH), BF16),
            pltpu.VMEM((tm + 2 * blk, 2 * KV_WIDTH), BF16),
            pltpu.VMEM((4, GQA_GROUP * blk, blk), F32),
        ],
        compiler_params=pltpu.CompilerParams(
            dimension_semantics=("arbitrary",),
            vmem_limit_bytes=VMEM_LIMIT_BYTES),
    )(sink, q, k, k, k, v, v, v, zs)


def _rope_tables(seq):
    freq = ROPE_THETA ** (-jnp.arange(0, ROPE_DIM, 2, dtype=F32) / ROPE_DIM)
    pos = jnp.arange(seq, dtype=F32)
    ang = freq[:, None] * pos[None, :]
    cos = jnp.cos(ang)
    sin = jnp.sin(ang)
    rest = HEAD_DIM - ROPE_DIM
    cos_t = jnp.concatenate([cos, cos, jnp.ones((rest, seq), F32)], axis=0)
    sin_t = jnp.concatenate([-sin, sin, jnp.zeros((rest, seq), F32)], axis=0)
    return cos_t, sin_t


def kernel(x_prompt, x_sample, p_prompt, p_sample, norm_g, a_w_in, a_w_grp, a_scale, a_w_out,
           b_w_in, b_q_norm, b_k_norm, b_sink, b_w_out, pe_w_proj, pe_norm_g, pe_w_gate):
    xs = (x_prompt, x_sample)
    group_seqs = tuple(x.shape[1] for x in xs)
    group_rows = tuple(x.shape[0] * x.shape[1] for x in xs)
    x2d = [x.reshape(rows, D_MODEL) for x, rows in zip(xs, group_rows)]
    p2d = [p.reshape(p.shape[0] * rows, PLE_DIM) for p, rows in zip((p_prompt, p_sample), group_rows)]

    a_w_ug, a_w_z = _fold_group_proj(a_w_in[0], a_w_grp[0])
    a_w_out_bf = a_w_out[0].astype(BF16)
    b_w_in_bf = b_w_in[0].astype(BF16)
    b_w_out_bf = b_w_out[0].astype(BF16)
    pe_w_proj_bf = pe_w_proj.astype(BF16)
    pe_w_gate_bf = pe_w_gate.astype(BF16)
    rope_cos, rope_sin = _rope_tables(max(group_seqs))

    t_sets = [[_pool_front(x, norm_g[0:1], a_w_ug, a_w_z, a_scale, seq=seq, part=part)
               for part in range(N_POOL_GROUPS // POOL_GROUPS_PER_CALL)]
              for x, seq in zip(x2d, group_seqs)]
    x1 = _out_pe(t_sets, a_w_out_bf, x2d, p2d, 0, pe_w_proj_bf[0], pe_norm_g[0:1], pe_w_gate_bf[0],
                 group_rows=group_rows)

    q, k, v, zs = _attn_front(x1, norm_g[1:2], b_w_in_bf, rope_cos, rope_sin, b_q_norm, b_k_norm,
                              group_rows=group_rows, group_seqs=group_seqs)
    og = _attn_core(b_sink[0], q, k, v, zs, group_rows=group_rows, group_seqs=group_seqs)
    ys = [_out_pe([[og]], b_w_out_bf, [x1], [p], 1, pe_w_proj_bf[1], pe_norm_g[1:2], pe_w_gate_bf[1],
                  group_rows=(rows,), combined_row_offset=sum(group_rows[:g]))
          for g, (p, rows) in enumerate(zip(p2d, group_rows))]
    return tuple(y.reshape(x.shape) for y, x in zip(ys, xs))
```

```python
import functools
import math

import jax
import jax.numpy as jnp
from jax import lax
from jax.experimental import pallas as pl
from jax.experimental.pallas import tpu as pltpu

F32 = jnp.float32
BF16 = jnp.bfloat16

D_MODEL = 2048
PLE_DIM = 256
EPS = 1e-6

POOL_WIDTH = 4096
POOL_WINDOWS = (2, 4, 8, 16)
N_POOL_GROUPS = len(POOL_WINDOWS)
POOL_GROUP_WIDTH = POOL_WIDTH // N_POOL_GROUPS
POOL_HALO = 8

HEAD_DIM = 128
N_Q_HEADS = 16
N_KV_HEADS = 4
GQA_GROUP = N_Q_HEADS // N_KV_HEADS
Q_WIDTH = N_Q_HEADS * HEAD_DIM
KV_WIDTH = N_KV_HEADS * HEAD_DIM
ATTN_IN_WIDTH = 2 * Q_WIDTH + 2 * KV_WIDTH
ATTN_BLOCK = 128
ROPE_DIM = HEAD_DIM // 4
ROPE_THETA = 500000.0
NEG_BIG = -1e30
LOG2_E = math.log2(math.e)

VMEM_LIMIT_BYTES = 56 * 1024 * 1024
LANES = 128
SUBLANES = 8

FOLD_ROW_BLOCK = 1024
POOL_TM = 512
POOL_GROUPS_PER_CALL = 2
POOL_ROW_BLOCK = 64
POOL_DIRECT_MAX = 8
ATTN_FRONT_TM = 512
ATTN_FRONT_TN = 1024
ATTN_FRONT_SUB = 256
ATTN_CORE_TM = 1024
OUT_TM_CHOICES = (512, 256)


def _rms_scale(x):
    return lax.rsqrt(jnp.mean(x * x, axis=-1, keepdims=True) + EPS)


def _dot(a, b):
    return jnp.dot(a, b, preferred_element_type=F32)


def _silu(z):
    return z * jax.nn.sigmoid(z)


def _tile_position(row0, group_rows, group_seqs):
    start = sum(group_rows[:-1])
    t0 = lax.rem(row0 - start, group_seqs[-1])
    seq = group_seqs[-1]
    for rows, s in zip(group_rows[-2::-1], group_seqs[-2::-1]):
        start -= rows
        inside = row0 < start + rows
        t0 = jnp.where(inside, lax.rem(row0 - start, s), t0)
        seq = jnp.where(inside, s, seq)
    return t0, seq


def _fold_group_kernel(wu_ref, wgrp_ref, wz_ref, wug_out_ref, wz_out_ref):
    wug_out_ref[...] = _dot(wu_ref[...].astype(BF16), wgrp_ref[0].astype(BF16)).astype(BF16)
    wz_out_ref[...] = wz_ref[...].astype(BF16)


def _fold_group_proj(w_in, w_grp):
    c = POOL_GROUP_WIDTH
    rows = FOLD_ROW_BLOCK
    out = jax.ShapeDtypeStruct((D_MODEL, POOL_WIDTH), BF16)
    return pl.pallas_call(
        _fold_group_kernel,
        name="fold_group_proj",
        grid=(N_POOL_GROUPS, D_MODEL // rows),
        in_specs=[
            pl.BlockSpec((rows, c), lambda g, r: (r, g)),
            pl.BlockSpec((1, c, c), lambda g, r: (g, 0, 0)),
            pl.BlockSpec((rows, c), lambda g, r: (r, N_POOL_GROUPS + g)),
        ],
        out_specs=[pl.BlockSpec((rows, c), lambda g, r: (r, g))] * 2,
        out_shape=[out, out],
        compiler_params=pltpu.CompilerParams(
            dimension_semantics=("arbitrary", "arbitrary"),
            vmem_limit_bytes=VMEM_LIMIT_BYTES),
    )(w_in, w_grp, w_in)


def _pool_front_kernel(x_ref, xprev_ref, xnext_ref, gain_ref, wug_ref, wz_ref,
                       scale_ref, t_ref, hn_ref, a_ref, zs_ref, h_ref, lvl_ref, *, tm, seq,
                       first_group, n_groups):
    i = pl.program_id(0)
    n_ext = tm + 2 * POOL_HALO
    c = POOL_GROUP_WIDTH
    n_levels = first_group + n_groups
    t0 = lax.rem(i * tm, seq)
    has_prev = t0 > 0
    has_next = t0 + tm < seq
    x = x_ref[...]
    r = _rms_scale(x)
    xp = xprev_ref[...]
    rp = jnp.where(has_prev, _rms_scale(xp), 0.0)
    xn = xnext_ref[...]
    rn = jnp.where(has_next, _rms_scale(xn), 0.0)
    edge = 2 * SUBLANES
    row = lax.broadcasted_iota(jnp.int32, (edge, 1), 0)
    inv_first, inv_last = [], []
    for win in POOL_WINDOWS[:n_levels]:
        half = win // 2
        for first_pos, out in ((t0 + row, inv_first), (t0 + tm - edge + row, inv_last)):
            cnt = jnp.minimum(first_pos + half, seq) - jnp.maximum(first_pos - half, 0)
            out.append(1.0 / cnt.astype(F32))

    pad = POOL_HALO
    top = pad + POOL_HALO
    n_buf = n_ext + 2 * pad
    n_chunks = D_MODEL // LANES
    for ch in range(n_chunks):
        cols = slice(ch * LANES, (ch + 1) * LANES)
        gain = gain_ref[:, cols]
        hc = x_ref[:, cols] * r * gain
        hn_ref[:, cols] = hc.astype(BF16)
        h_ref[top:top + tm, cols] = hc
        h_ref[pad:top, cols] = xprev_ref[:, cols] * rp * gain
        h_ref[top + tm:top + tm + POOL_HALO, cols] = xnext_ref[:, cols] * rn * gain
    h_ref[0:pad, :] = jnp.zeros((pad, D_MODEL), F32)
    h_ref[n_buf - pad:, :] = jnp.zeros((pad, D_MODEL), F32)
    for k in range(n_groups):
        gcols = slice(k * c, (k + 1) * c)
        zs_ref[:, gcols] = _silu(_dot(hn_ref[...], wz_ref[:, gcols]))

    centre_blocks = [(top + b, top + b + POOL_ROW_BLOCK) for b in range(0, tm, POOL_ROW_BLOCK)]
    all_blocks = [(pad, top)] + centre_blocks + [(top + tm, top + tm + POOL_HALO)]
    wins = POOL_WINDOWS[first_group:first_group + n_groups]
    n_sets = lvl_ref.shape[0]
    for ch in range(n_chunks):
        cols = slice(ch * LANES, (ch + 1) * LANES)
        lvl = lvl_ref.at[ch % n_sets]
        for k, win in enumerate(wins):
            gi = first_group + k
            feeds_next = k + 1 < n_groups and wins[k + 1] > POOL_DIRECT_MAX
            for b0, b1 in (all_blocks if feeds_next else centre_blocks):
                if win <= POOL_DIRECT_MAX:
                    terms = [h_ref[b0 + u:b1 + u, cols] for u in range(-(win // 2), win // 2)]
                    while len(terms) > 1:
                        terms = [terms[j] + terms[j + 1] for j in range(0, len(terms), 2)]
                    level = terms[0]
                else:
                    assert win == 2 * wins[k - 1] and wins[k - 1] <= POOL_DIRECT_MAX
                    level = lvl[b0 - win // 4:b1 - win // 4, :] + lvl[b0 + win // 4:b1 + win // 4, :]
                if feeds_next:
                    lvl[b0:b1, :] = level
                r0, r1 = b0 - top, b1 - top
                if r0 < 0 or r1 > tm:
                    continue
                h = h_ref[b0:b1, cols]
                inv = 1.0 / win
                if r0 == 0:
                    a_ref[k, 0:edge, cols] = (level[:edge] * inv_first[gi] - h[:edge]).astype(BF16)
                    a_ref[k, edge:r1, cols] = (level[edge:] * inv - h[edge:]).astype(BF16)
                elif r1 == tm:
                    a_ref[k, r0:tm - edge, cols] = (level[:-edge] * inv - h[:-edge]).astype(BF16)
                    a_ref[k, tm - edge:tm, cols] = (level[-edge:] * inv_last[gi] - h[-edge:]
                                                    ).astype(BF16)
                else:
                    a_ref[k, r0:r1, cols] = (level * inv - h).astype(BF16)

    for k in range(n_groups):
        gcols = slice(k * c, (k + 1) * c)
        mm = _dot(a_ref[k], wug_ref[:, gcols])
        t_ref[:, gcols] = ((mm * scale_ref[:, gcols]) * zs_ref[:, gcols]).astype(BF16)


def _pool_front(x2d, gain, w_ug, w_z, scale, *, seq, part):
    rows = x2d.shape[0]
    tm = POOL_TM
    n_groups = POOL_GROUPS_PER_CALL
    width = n_groups * POOL_GROUP_WIDTH
    n_row_blocks8 = rows // SUBLANES
    resident = pl.Buffered(1)
    kernel = functools.partial(_pool_front_kernel, tm=tm, seq=seq, first_group=part * n_groups,
                               n_groups=n_groups)
    return pl.pallas_call(
        kernel,
        name="pool_front",
        grid=(rows // tm,),
        in_specs=[
            pl.BlockSpec((tm, D_MODEL), lambda i: (i, 0)),
            pl.BlockSpec((POOL_HALO, D_MODEL),
                         lambda i: (jnp.maximum(i * (tm // SUBLANES) - 1, 0), 0)),
            pl.BlockSpec((POOL_HALO, D_MODEL),
                         lambda i: (jnp.minimum((i + 1) * (tm // SUBLANES), n_row_blocks8 - 1), 0)),
            pl.BlockSpec((1, D_MODEL), lambda i: (0, 0)),
            pl.BlockSpec((D_MODEL, width), lambda i: (0, part), pipeline_mode=resident),
            pl.BlockSpec((D_MODEL, width), lambda i: (0, part), pipeline_mode=resident),
            pl.BlockSpec((1, width), lambda i: (0, part)),
        ],
        out_specs=pl.BlockSpec((tm, width), lambda i: (i, 0)),
        out_shape=jax.ShapeDtypeStruct((rows, width), BF16),
        scratch_shapes=[
            pltpu.VMEM((tm, D_MODEL), BF16),
            pltpu.VMEM((n_groups, tm, D_MODEL), BF16),
            pltpu.VMEM((tm, width), F32),
            pltpu.VMEM((tm + 4 * POOL_HALO, D_MODEL), F32),
            pltpu.VMEM((2, tm + 4 * POOL_HALO, LANES), F32),
        ],
        compiler_params=pltpu.CompilerParams(
            dimension_semantics=("arbitrary",),
            vmem_limit_bytes=VMEM_LIMIT_BYTES),
    )(x2d, x2d, x2d, gain, w_ug, w_z, scale)


def _out_pe_math(t_refs, wout_ref, x_ref, p_ref, wp_ref, gain_ref, wgate_ref, out_ref):
    y = None
    k0 = 0
    for t_ref in t_refs:
        k1 = k0 + t_ref.shape[1]
        part = _dot(t_ref[...], wout_ref[k0:k1, :])
        y = part if y is None else y + part
        k0 = k1
    h = x_ref[...] + y
    hn = (h * _rms_scale(h) * gain_ref[...]).astype(BF16)
    gate = jax.nn.sigmoid(_dot(hn, wgate_ref[...]))
    e = _dot(p_ref[...].astype(BF16), wp_ref[...])
    out_ref[...] = h + e * gate


def _out_pe_kernel(*refs, n_parts, group_tiles, per_group):
    n_groups = len(group_tiles)
    n_sources = n_groups if per_group else 1
    refs = list(refs)

    def take(count):
        taken = refs[:count]
        del refs[:count]
        return taken

    t_sets = [take(n_parts) for _ in range(n_sources)]
    x_refs = take(n_sources)
    p_refs = take(n_groups)
    wout_ref, wp_ref, gain_ref, wgate_ref, out_ref = take(5)
    assert not refs

    i = pl.program_id(0)
    start = 0
    for g, tiles in enumerate(group_tiles):
        src = g if per_group else 0
        body = functools.partial(_out_pe_math, t_sets[src], wout_ref, x_refs[src], p_refs[g], wp_ref,
                                 gain_ref, wgate_ref, out_ref)
        pl.when((i >= start) & (i < start + tiles))(body)
        start += tiles


def _out_pe_tile(k, n_t, n_x, n_p):
    weights = 2 * (k * D_MODEL + PLE_DIM * D_MODEL + D_MODEL * D_MODEL)
    for tm in OUT_TM_CHOICES:
        row_blocks = 2 * tm * (n_t * 2 * k + n_x * 4 * D_MODEL + n_p * 4 * PLE_DIM + 4 * D_MODEL)
        temporaries = 2 * tm * D_MODEL * 4
        if weights + row_blocks + temporaries <= VMEM_LIMIT_BYTES:
            return tm
    raise ValueError(f"no out_pe row tile fits VMEM for k={k}")


def _out_pe(t_sets, w_out, x_list, p_list, layer, w_proj, gain, w_gate, *, group_rows,
            combined_row_offset=None):
    n_groups = len(group_rows)
    n_parts = len(t_sets[0])
    k = sum(t.shape[1] for t in t_sets[0])
    assert k == w_out.shape[0] and len(p_list) == n_groups
    tm = _out_pe_tile(k, len(t_sets), len(x_list), n_groups)
    group_tiles = tuple(rows // tm for rows in group_rows)
    starts = tuple(sum(group_tiles[:g]) for g in range(n_groups))
    total_rows = sum(group_rows)
    resident = pl.Buffered(1)

    per_group = combined_row_offset is None
    assert len(t_sets) == len(x_list) == (n_groups if per_group else 1)
    offset_tiles = 0 if per_group else combined_row_offset // tm

    def combined(width, offset=offset_tiles):
        return pl.BlockSpec((tm, width), lambda i: (i + offset, 0))

    def grouped(width, g, block_offset=0):
        lo, n = starts[g], group_tiles[g]
        return pl.BlockSpec((tm, width), lambda i: (jnp.clip(i - lo, 0, n - 1) + block_offset, 0))

    in_specs, operands = [], []
    for g, parts in enumerate(t_sets):
        for t in parts:
            in_specs.append(grouped(t.shape[1], g) if per_group else combined(t.shape[1]))
            operands.append(t)
    for g, x in enumerate(x_list):
        in_specs.append(grouped(D_MODEL, g) if per_group else combined(D_MODEL))
        operands.append(x)
    for g, p in enumerate(p_list):
        in_specs.append(grouped(PLE_DIM, g, block_offset=layer * group_tiles[g]))
        operands.append(p)
    in_specs += [
        pl.BlockSpec((k, D_MODEL), lambda i: (0, 0), pipeline_mode=resident),
        pl.BlockSpec((PLE_DIM, D_MODEL), lambda i: (0, 0), pipeline_mode=resident),
        pl.BlockSpec((1, D_MODEL), lambda i: (0, 0)),
        pl.BlockSpec((D_MODEL, D_MODEL), lambda i: (0, 0), pipeline_mode=resident),
    ]
    operands += [w_out, w_proj, gain, w_gate]
    kernel = functools.partial(_out_pe_kernel, n_parts=n_parts, group_tiles=group_tiles,
                               per_group=per_group)
    return pl.pallas_call(
        kernel,
        name="out_pe",
        grid=(total_rows // tm,),
        in_specs=in_specs,
        out_specs=combined(D_MODEL, offset=0),
        out_shape=jax.ShapeDtypeStruct((total_rows, D_MODEL), F32),
        compiler_params=pltpu.CompilerParams(
            dimension_semantics=("arbitrary",),
            vmem_limit_bytes=VMEM_LIMIT_BYTES),
    )(*operands)


def _attn_front_kernel(x_ref, gain_ref, w_ref, cos_ref, sin_ref, qg_ref, kg_ref,
                       q_ref, k_ref, v_ref, zs_ref, hn_ref):
    tm = x_ref.shape[0]
    half = ROPE_DIM // 2
    tn = ATTN_FRONT_TN
    heads_per_block = tn // HEAD_DIM

    for r0 in range(0, tm, ATTN_FRONT_SUB):
        rows = slice(r0, r0 + ATTN_FRONT_SUB)
        x = x_ref[rows, :]
        hn_ref[rows, :] = (x * _rms_scale(x) * gain_ref[...]).astype(BF16)
        cos = cos_ref[:, rows].T
        sin = sin_ref[:, rows].T
        first_half = lax.broadcasted_iota(jnp.int32, cos.shape, 1) < half

        def norm_rope(slab, head_gain, cos=cos, sin=sin, first_half=first_half):
            s = slab * _rms_scale(slab) * head_gain
            partner = jnp.where(first_half, pltpu.roll(s, HEAD_DIM - half, axis=1),
                                pltpu.roll(s, half, axis=1))
            return s * cos + partner * sin

        for n in range(ATTN_IN_WIDTH // tn):
            y = _dot(hn_ref[rows, :], w_ref[:, n * tn:(n + 1) * tn])
            if n < Q_WIDTH // tn:
                for h in range(heads_per_block):
                    cols = slice(h * HEAD_DIM, (h + 1) * HEAD_DIM)
                    q_ref[rows, n * tn + h * HEAD_DIM:n * tn + (h + 1) * HEAD_DIM] = (
                        norm_rope(y[:, cols], qg_ref[...]).astype(BF16))
            elif n == Q_WIDTH // tn:
                for h in range(N_KV_HEADS):
                    cols = slice(h * HEAD_DIM, (h + 1) * HEAD_DIM)
                    k_ref[rows, cols] = norm_rope(y[:, cols], kg_ref[...]).astype(BF16)
                v_ref[rows, :] = y[:, KV_WIDTH:].astype(BF16)
            else:
                z0 = (n - Q_WIDTH // tn - 1) * tn
                zs_ref[rows, z0:z0 + tn] = _silu(y)


def _attn_front(x2d, gain, w_in, cos_t, sin_t, q_gain, k_gain, *, group_rows, group_seqs):
    rows = x2d.shape[0]
    tm = ATTN_FRONT_TM
    assert 2 * KV_WIDTH == ATTN_FRONT_TN

    def table_map(i):
        t0, _ = _tile_position(i * tm, group_rows, group_seqs)
        return (0, t0 // tm)

    table_spec = pl.BlockSpec((HEAD_DIM, tm), table_map)
    row_spec = lambda width: pl.BlockSpec((tm, width), lambda i: (i, 0))
    return pl.pallas_call(
        _attn_front_kernel,
        name="attn_front",
        grid=(rows // tm,),
        in_specs=[
            row_spec(D_MODEL),
            pl.BlockSpec((1, D_MODEL), lambda i: (0, 0)),
            pl.BlockSpec((D_MODEL, ATTN_IN_WIDTH), lambda i: (0, 0), pipeline_mode=pl.Buffered(1)),
            table_spec, table_spec,
            pl.BlockSpec((1, HEAD_DIM), lambda i: (0, 0)),
            pl.BlockSpec((1, HEAD_DIM), lambda i: (0, 0)),
        ],
        out_specs=[row_spec(Q_WIDTH), row_spec(KV_WIDTH), row_spec(KV_WIDTH), row_spec(Q_WIDTH)],
        out_shape=[
            jax.ShapeDtypeStruct((rows, Q_WIDTH), BF16),
            jax.ShapeDtypeStruct((rows, KV_WIDTH), BF16),
            jax.ShapeDtypeStruct((rows, KV_WIDTH), BF16),
            jax.ShapeDtypeStruct((rows, Q_WIDTH), F32),
        ],
        scratch_shapes=[pltpu.VMEM((tm, D_MODEL), BF16)],
        compiler_params=pltpu.CompilerParams(
            dimension_semantics=("arbitrary",),
            vmem_limit_bytes=VMEM_LIMIT_BYTES),
    )(x2d, gain, w_in, cos_t, sin_t, q_gain, k_gain)


def _attn_core_kernel(sink_ref, q_ref, kprev_ref, kmid_ref, knext_ref, vprev_ref, vmid_ref,
                      vnext_ref, zs_ref, o_ref, kbuf_ref, vbuf_ref, bias_ref, *, tm, group_rows,
                      group_seqs):
    i = pl.program_id(0)
    blk = ATTN_BLOCK
    n_qb = tm // blk
    t0, seq = _tile_position(i * tm, group_rows, group_seqs)
    kbuf_ref[0:blk, :] = kprev_ref[...]
    kbuf_ref[blk:blk + tm, :] = kmid_ref[...]
    kbuf_ref[blk + tm:, :] = knext_ref[...]
    for h in range(N_KV_HEADS):
        src = slice(h * HEAD_DIM, (h + 1) * HEAD_DIM)
        dst = slice(2 * h * HEAD_DIM, (2 * h + 1) * HEAD_DIM)
        vbuf_ref[0:blk, dst] = vprev_ref[:, src]
        vbuf_ref[blk:blk + tm, dst] = vmid_ref[:, src]
        vbuf_ref[blk + tm:, dst] = vnext_ref[:, src]
        vbuf_ref[:, (2 * h + 1) * HEAD_DIM:(2 * h + 2) * HEAD_DIM] = jnp.ones(
            (tm + 2 * blk, HEAD_DIM), BF16)

    n_rows = GQA_GROUP * blk
    qi = lax.rem(lax.broadcasted_iota(jnp.int32, (n_rows, blk), 0), blk)
    kj = lax.broadcasted_iota(jnp.int32, (n_rows, blk), 1)
    prev_bias = jnp.where(kj >= qi, 0.0, NEG_BIG).astype(F32)
    next_bias = jnp.where(kj <= qi, 0.0, NEG_BIG).astype(F32)
    bias_ref[0] = jnp.where(t0 == 0, NEG_BIG, prev_bias)
    bias_ref[1] = prev_bias
    bias_ref[2] = jnp.where(t0 + tm == seq, NEG_BIG, next_bias)
    bias_ref[3] = next_bias

    sqrt_d = math.sqrt(HEAD_DIM)
    exp2_scale = LOG2_E / sqrt_d

    for qb in range(n_qb):
        row0 = qb * blk
        rows = pl.ds(row0, blk)
        b_prev = bias_ref[0 if qb == 0 else 1]
        b_next = bias_ref[2 if qb == n_qb - 1 else 3]
        for kvh in range(N_KV_HEADS):
            heads = [kvh * GQA_GROUP + hg for hg in range(GQA_GROUP)]
            qs = jnp.concatenate(
                [q_ref[rows, h * HEAD_DIM:(h + 1) * HEAD_DIM] for h in heads], axis=0)
            kw = kbuf_ref[pl.ds(row0, 3 * blk), kvh * HEAD_DIM:(kvh + 1) * HEAD_DIM]
            vw = vbuf_ref[pl.ds(row0, 3 * blk), 2 * kvh * HEAD_DIM:(2 * kvh + 2) * HEAD_DIM]
            s = lax.dot_general(qs, kw, (((1,), (1,)), ((), ())), preferred_element_type=F32)
            sb = (s[:, 0:blk] + b_prev, s[:, blk:2 * blk], s[:, 2 * blk:] + b_next)
            row_max = jnp.max(jnp.maximum(jnp.maximum(sb[0], sb[1]), sb[2]), axis=-1, keepdims=True)
            e_rows, m_rows, sinks = [], [], []
            for hg, h in enumerate(heads):
                r = slice(hg * blk, (hg + 1) * blk)
                sink_raw = sink_ref[h] * sqrt_d
                m_h = jnp.maximum(row_max[r], sink_raw)
                e_rows.append(jnp.concatenate(
                    [jnp.exp2((b[r] - m_h) * exp2_scale).astype(BF16) for b in sb], axis=1))
                m_rows.append(m_h)
                sinks.append(sink_raw)
            e = jnp.concatenate(e_rows, axis=0)
            ov = _dot(e, vw)
            for hg, h in enumerate(heads):
                r = slice(hg * blk, (hg + 1) * blk)
                hcols = slice(h * HEAD_DIM, (h + 1) * HEAD_DIM)
                denom = ov[r, HEAD_DIM:] + jnp.exp2((sinks[hg] - m_rows[hg]) * exp2_scale)
                o_ref[rows, hcols] = (ov[r, :HEAD_DIM] * (1.0 / denom) * zs_ref[rows, hcols]
                                      ).astype(BF16)


def _attn_core(sink, q, k, v, zs, *, group_rows, group_seqs):
    rows = q.shape[0]
    tm = ATTN_CORE_TM
    blk = ATTN_BLOCK
    per = tm // blk
    n_blocks = rows // blk
    prev_map = lambda i: (jnp.maximum(i * per - 1, 0), 0)
    next_map = lambda i: (jnp.minimum((i + 1) * per, n_blocks - 1), 0)
    mid_map = lambda i: (i, 0)
    kernel = functools.partial(_attn_core_kernel, tm=tm, group_rows=group_rows,
                               group_seqs=group_seqs)
    return pl.pallas_call(
        kernel,
        name="attn_core",
        grid=(rows // tm,),
        in_specs=[
            pl.BlockSpec(memory_space=pltpu.SMEM),
            pl.BlockSpec((tm, Q_WIDTH), mid_map),
            pl.BlockSpec((blk, KV_WIDTH), prev_map),
            pl.BlockSpec((tm, KV_WIDTH), mid_map),
            pl.BlockSpec((blk, KV_WIDTH), next_map),
            pl.BlockSpec((blk, KV_WIDTH), prev_map),
            pl.BlockSpec((tm, KV_WIDTH), mid_map),
            pl.BlockSpec((blk, KV_WIDTH), next_map),
            pl.BlockSpec((tm, Q_WIDTH), mid_map),
        ],
        out_specs=pl.BlockSpec((tm, Q_WIDTH), mid_map),
        out_shape=jax.ShapeDtypeStruct((rows, Q_WIDTH), BF16),
        scratch_shapes=[
            pltpu.VMEM((tm + 2 * blk, KV_WIDTH), BF16),
            pltpu.VMEM((tm + 2 * blk, 2 * KV_WIDTH), BF16),
            pltpu.VMEM((4, GQA_GROUP * blk, blk), F32),
        ],
        compiler_params=pltpu.CompilerParams(
            dimension_semantics=("arbitrary",),
            vmem_limit_bytes=VMEM_LIMIT_BYTES),
    )(sink, q, k, k, k, v, v, v, zs)


def _rope_tables(seq):
    freq = ROPE_THETA ** (-jnp.arange(0, ROPE_DIM, 2, dtype=F32) / ROPE_DIM)
    pos = jnp.arange(seq, dtype=F32)
    ang = freq[:, None] * pos[None, :]
    cos = jnp.cos(ang)
    sin = jnp.sin(ang)
    rest = HEAD_DIM - ROPE_DIM
    cos_t = jnp.concatenate([cos, cos, jnp.ones((rest, seq), F32)], axis=0)
    sin_t = jnp.concatenate([-sin, sin, jnp.zeros((rest, seq), F32)], axis=0)
    return cos_t, sin_t


def kernel(x_prompt, x_sample, p_prompt, p_sample, norm_g, a_w_in, a_w_grp, a_scale, a_w_out,
           b_w_in, b_q_norm, b_k_norm, b_sink, b_w_out, pe_w_proj, pe_norm_g, pe_w_gate):
    xs = (x_prompt, x_sample)
    group_seqs = tuple(x.shape[1] for x in xs)
    group_rows = tuple(x.shape[0] * x.shape[1] for x in xs)
    x2d = [x.reshape(rows, D_MODEL) for x, rows in zip(xs, group_rows)]
    p2d = [p.reshape(p.shape[0] * rows, PLE_DIM) for p, rows in zip((p_prompt, p_sample), group_rows)]

    a_w_ug, a_w_z = _fold_group_proj(a_w_in[0], a_w_grp[0])
    a_w_out_bf = a_w_out[0].astype(BF16)
    b_w_in_bf = b_w_in[0].astype(BF16)
    b_w_out_bf = b_w_out[0].astype(BF16)
    pe_w_proj_bf = pe_w_proj.astype(BF16)
    pe_w_gate_bf = pe_w_gate.astype(BF16)
    rope_cos, rope_sin = _rope_tables(max(group_seqs))

    t_sets = [[_pool_front(x, norm_g[0:1], a_w_ug, a_w_z, a_scale, seq=seq, part=part)
               for part in range(N_POOL_GROUPS // POOL_GROUPS_PER_CALL)]
              for x, seq in zip(x2d, group_seqs)]
    x1 = _out_pe(t_sets, a_w_out_bf, x2d, p2d, 0, pe_w_proj_bf[0], pe_norm_g[0:1], pe_w_gate_bf[0],
                 group_rows=group_rows)

    q, k, v, zs = _attn_front(x1, norm_g[1:2], b_w_in_bf, rope_cos, rope_sin, b_q_norm, b_k_norm,
                              group_rows=group_rows, group_seqs=group_seqs)
    og = _attn_core(b_sink[0], q, k, v, zs, group_rows=group_rows, group_seqs=group_seqs)
    ys = [_out_pe([[og]], b_w_out_bf, [x1], [p], 1, pe_w_proj_bf[1], pe_norm_g[1:2], pe_w_gate_bf[1],
                  group_rows=(rows,), combined_row_offset=sum(group_rows[:g]))
          for g, (p, rows) in enumerate(zip(p2d, group_rows))]
    return tuple(y.reshape(x.shape) for y, x in zip(ys, xs))
```

```python
import functools
import math

import jax
import jax.numpy as jnp
from jax import lax
from jax.experimental import pallas as pl
from jax.experimental.pallas import tpu as pltpu

F32 = jnp.float32
BF16 = jnp.bfloat16

D_MODEL = 2048
PLE_DIM = 256
EPS = 1e-6

POOL_WIDTH = 4096
POOL_WINDOWS = (2, 4, 8, 16)
N_POOL_GROUPS = len(POOL_WINDOWS)
POOL_GROUP_WIDTH = POOL_WIDTH // N_POOL_GROUPS
POOL_HALO = 8

HEAD_DIM = 128
N_Q_HEADS = 16
N_KV_HEADS = 4
GQA_GROUP = N_Q_HEADS // N_KV_HEADS
Q_WIDTH = N_Q_HEADS * HEAD_DIM
KV_WIDTH = N_KV_HEADS * HEAD_DIM
ATTN_IN_WIDTH = 2 * Q_WIDTH + 2 * KV_WIDTH
ATTN_BLOCK = 128
ROPE_DIM = HEAD_DIM // 4
ROPE_THETA = 500000.0
NEG_BIG = -1e30
LOG2_E = math.log2(math.e)

VMEM_LIMIT_BYTES = 56 * 1024 * 1024
LANES = 128
SUBLANES = 8

FOLD_ROW_BLOCK = 1024
POOL_TM = 512
POOL_GROUPS_PER_CALL = 2
POOL_ROW_BLOCK = 64
POOL_DIRECT_MAX = 8
ATTN_FRONT_TM = 512
ATTN_FRONT_TN = 1024
ATTN_FRONT_SUB = 256
ATTN_CORE_TM = 1024
OUT_TM_CHOICES = (512, 256)


def _rms_scale(x):
    return lax.rsqrt(jnp.mean(x * x, axis=-1, keepdims=True) + EPS)


def _dot(a, b):
    return jnp.dot(a, b, preferred_element_type=F32)


def _silu(z):
    return z * jax.nn.sigmoid(z)


def _tile_position(row0, group_rows, group_seqs):
    start = sum(group_rows[:-1])
    t0 = lax.rem(row0 - start, group_seqs[-1])
    seq = group_seqs[-1]
    for rows, s in zip(group_rows[-2::-1], group_seqs[-2::-1]):
        start -= rows
        inside = row0 < start + rows
        t0 = jnp.where(inside, lax.rem(row0 - start, s), t0)
        seq = jnp.where(inside, s, seq)
    return t0, seq


def _fold_group_kernel(wu_ref, wgrp_ref, wz_ref, wug_out_ref, wz_out_ref):
    wug_out_ref[...] = _dot(wu_ref[...].astype(BF16), wgrp_ref[0].astype(BF16)).astype(BF16)
    wz_out_ref[...] = wz_ref[...].astype(BF16)


def _fold_group_proj(w_in, w_grp):
    c = POOL_GROUP_WIDTH
    rows = FOLD_ROW_BLOCK
    out = jax.ShapeDtypeStruct((D_MODEL, POOL_WIDTH), BF16)
    return pl.pallas_call(
        _fold_group_kernel,
        name="fold_group_proj",
        grid=(N_POOL_GROUPS, D_MODEL // rows),
        in_specs=[
            pl.BlockSpec((rows, c), lambda g, r: (r, g)),
            pl.BlockSpec((1, c, c), lambda g, r: (g, 0, 0)),
            pl.BlockSpec((rows, c), lambda g, r: (r, N_POOL_GROUPS + g)),
        ],
        out_specs=[pl.BlockSpec((rows, c), lambda g, r: (r, g))] * 2,
        out_shape=[out, out],
        compiler_params=pltpu.CompilerParams(
            dimension_semantics=("arbitrary", "arbitrary"),
            vmem_limit_bytes=VMEM_LIMIT_BYTES),
    )(w_in, w_grp, w_in)


def _pool_front_kernel(x_ref, xprev_ref, xnext_ref, gain_ref, wug_ref, wz_ref,
                       scale_ref, t_ref, hn_ref, a_ref, zs_ref, h_ref, lvl_ref, *, tm, seq,
                       first_group, n_groups):
    i = pl.program_id(0)
    n_ext = tm + 2 * POOL_HALO
    c = POOL_GROUP_WIDTH
    n_levels = first_group + n_groups
    t0 = lax.rem(i * tm, seq)
    has_prev = t0 > 0
    has_next = t0 + tm < seq
    x = x_ref[...]
    r = _rms_scale(x)
    xp = xprev_ref[...]
    rp = jnp.where(has_prev, _rms_scale(xp), 0.0)
    xn = xnext_ref[...]
    rn = jnp.where(has_next, _rms_scale(xn), 0.0)
    edge = 2 * SUBLANES
    row = lax.broadcasted_iota(jnp.int32, (edge, 1), 0)
    inv_first, inv_last = [], []
    for win in POOL_WINDOWS[:n_levels]:
        half = win // 2
        for first_pos, out in ((t0 + row, inv_first), (t0 + tm - edge + row, inv_last)):
            cnt = jnp.minimum(first_pos + half, seq) - jnp.maximum(first_pos - half, 0)
            out.append(1.0 / cnt.astype(F32))

    pad = POOL_HALO
    top = pad + POOL_HALO
    n_buf = n_ext + 2 * pad
    n_chunks = D_MODEL // LANES
    for ch in range(n_chunks):
        cols = slice(ch * LANES, (ch + 1) * LANES)
        gain = gain_ref[:, cols]
        hc = x_ref[:, cols] * r * gain
        hn_ref[:, cols] = hc.astype(BF16)
        h_ref[top:top + tm, cols] = hc
        h_ref[pad:top, cols] = xprev_ref[:, cols] * rp * gain
        h_ref[top + tm:top + tm + POOL_HALO, cols] = xnext_ref[:, cols] * rn * gain
    h_ref[0:pad, :] = jnp.zeros((pad, D_MODEL), F32)
    h_ref[n_buf - pad:, :] = jnp.zeros((pad, D_MODEL), F32)
    for k in range(n_groups):
        gcols = slice(k * c, (k + 1) * c)
        zs_ref[:, gcols] = _silu(_dot(hn_ref[...], wz_ref[:, gcols]))

    centre_blocks = [(top + b, top + b + POOL_ROW_BLOCK) for b in range(0, tm, POOL_ROW_BLOCK)]
    all_blocks = [(pad, top)] + centre_blocks + [(top + tm, top + tm + POOL_HALO)]
    wins = POOL_WINDOWS[first_group:first_group + n_groups]
    n_sets = lvl_ref.shape[0]
    for ch in range(n_chunks):
        cols = slice(ch * LANES, (ch + 1) * LANES)
        lvl = lvl_ref.at[ch % n_sets]
        for k, win in enumerate(wins):
            gi = first_group + k
            feeds_next = k + 1 < n_groups and wins[k + 1] > POOL_DIRECT_MAX
            for b0, b1 in (all_blocks if feeds_next else centre_blocks):
                if win <= POOL_DIRECT_MAX:
                    terms = [h_ref[b0 + u:b1 + u, cols] for u in range(-(win // 2), win // 2)]
                    while len(terms) > 1:
                        terms = [terms[j] + terms[j + 1] for j in range(0, len(terms), 2)]
                    level = terms[0]
                else:
                    assert win == 2 * wins[k - 1] and wins[k - 1] <= POOL_DIRECT_MAX
                    level = lvl[b0 - win // 4:b1 - win // 4, :] + lvl[b0 + win // 4:b1 + win // 4, :]
                if feeds_next:
                    lvl[b0:b1, :] = level
                r0, r1 = b0 - top, b1 - top
                if r0 < 0 or r1 > tm:
                    continue
                h = h_ref[b0:b1, cols]
                inv = 1.0 / win
                if r0 == 0:
                    a_ref[k, 0:edge, cols] = (level[:edge] * inv_first[gi] - h[:edge]).astype(BF16)
                    a_ref[k, edge:r1, cols] = (level[edge:] * inv - h[edge:]).astype(BF16)
                elif r1 == tm:
                    a_ref[k, r0:tm - edge, cols] = (level[:-edge] * inv - h[:-edge]).astype(BF16)
                    a_ref[k, tm - edge:tm, cols] = (level[-edge:] * inv_last[gi] - h[-edge:]
                                                    ).astype(BF16)
                else:
                    a_ref[k, r0:r1, cols] = (level * inv - h).astype(BF16)

    for k in range(n_groups):
        gcols = slice(k * c, (k + 1) * c)
        mm = _dot(a_ref[k], wug_ref[:, gcols])
        t_ref[:, gcols] = ((mm * scale_ref[:, gcols]) * zs_ref[:, gcols]).astype(BF16)


def _pool_front(x2d, gain, w_ug, w_z, scale, *, seq, part):
    rows = x2d.shape[0]
    tm = POOL_TM
    n_groups = POOL_GROUPS_PER_CALL
    width = n_groups * POOL_GROUP_WIDTH
    n_row_blocks8 = rows // SUBLANES
    resident = pl.Buffered(1)
    kernel = functools.partial(_pool_front_kernel, tm=tm, seq=seq, first_group=part * n_groups,
                               n_groups=n_groups)
    return pl.pallas_call(
        kernel,
        name="pool_front",
        grid=(rows // tm,),
        in_specs=[
            pl.BlockSpec((tm, D_MODEL), lambda i: (i, 0)),
            pl.BlockSpec((POOL_HALO, D_MODEL),
                         lambda i: (jnp.maximum(i * (tm // SUBLANES) - 1, 0), 0)),
            pl.BlockSpec((POOL_HALO, D_MODEL),
                         lambda i: (jnp.minimum((i + 1) * (tm // SUBLANES), n_row_blocks8 - 1), 0)),
            pl.BlockSpec((1, D_MODEL), lambda i: (0, 0)),
            pl.BlockSpec((D_MODEL, width), lambda i: (0, part), pipeline_mode=resident),
            pl.BlockSpec((D_MODEL, width), lambda i: (0, part), pipeline_mode=resident),
            pl.BlockSpec((1, width), lambda i: (0, part)),
        ],
        out_specs=pl.BlockSpec((tm, width), lambda i: (i, 0)),
        out_shape=jax.ShapeDtypeStruct((rows, width), BF16),
        scratch_shapes=[
            pltpu.VMEM((tm, D_MODEL), BF16),
            pltpu.VMEM((n_groups, tm, D_MODEL), BF16),
            pltpu.VMEM((tm, width), F32),
            pltpu.VMEM((tm + 4 * POOL_HALO, D_MODEL), F32),
            pltpu.VMEM((2, tm + 4 * POOL_HALO, LANES), F32),
        ],
        compiler_params=pltpu.CompilerParams(
            dimension_semantics=("arbitrary",),
            vmem_limit_bytes=VMEM_LIMIT_BYTES),
    )(x2d, x2d, x2d, gain, w_ug, w_z, scale)


def _out_pe_math(t_refs, wout_ref, x_ref, p_ref, wp_ref, gain_ref, wgate_ref, out_ref):
    y = None
    k0 = 0
    for t_ref in t_refs:
        k1 = k0 + t_ref.shape[1]
        part = _dot(t_ref[...], wout_ref[k0:k1, :])
        y = part if y is None else y + part
        k0 = k1
    h = x_ref[...] + y
    hn = (h * _rms_scale(h) * gain_ref[...]).astype(BF16)
    gate = jax.nn.sigmoid(_dot(hn, wgate_ref[...]))
    e = _dot(p_ref[...].astype(BF16), wp_ref[...])
    out_ref[...] = h + e * gate


def _out_pe_kernel(*refs, n_parts, group_tiles, per_group):
    n_groups = len(group_tiles)
    n_sources = n_groups if per_group else 1
    refs = list(refs)

    def take(count):
        taken = refs[:count]
        del refs[:count]
        return taken

    t_sets = [take(n_parts) for _ in range(n_sources)]
    x_refs = take(n_sources)
    p_refs = take(n_groups)
    wout_ref, wp_ref, gain_ref, wgate_ref, out_ref = take(5)
    assert not refs

    i = pl.program_id(0)
    start = 0
    for g, tiles in enumerate(group_tiles):
        src = g if per_group else 0
        body = functools.partial(_out_pe_math, t_sets[src], wout_ref, x_refs[src], p_refs[g], wp_ref,
                                 gain_ref, wgate_ref, out_ref)
        pl.when((i >= start) & (i < start + tiles))(body)
        start += tiles


def _out_pe_tile(k, n_t, n_x, n_p):
    weights = 2 * (k * D_MODEL + PLE_DIM * D_MODEL + D_MODEL * D_MODEL)
    for tm in OUT_TM_CHOICES:
        row_blocks = 2 * tm * (n_t * 2 * k + n_x * 4 * D_MODEL + n_p * 4 * PLE_DIM + 4 * D_MODEL)
        temporaries = 2 * tm * D_MODEL * 4
        if weights + row_blocks + temporaries <= VMEM_LIMIT_BYTES:
            return tm
    raise ValueError(f"no out_pe row tile fits VMEM for k={k}")


def _out_pe(t_sets, w_out, x_list, p_list, layer, w_proj, gain, w_gate, *, group_rows,
            combined_row_offset=None):
    n_groups = len(group_rows)
    n_parts = len(t_sets[0])
    k = sum(t.shape[1] for t in t_sets[0])
    assert k == w_out.shape[0] and len(p_list) == n_groups
    tm = _out_pe_tile(k, len(t_sets), len(x_list), n_groups)
    group_tiles = tuple(rows // tm for rows in group_rows)
    starts = tuple(sum(group_tiles[:g]) for g in range(n_groups))
    total_rows = sum(group_rows)
    resident = pl.Buffered(1)

    per_group = combined_row_offset is None
    assert len(t_sets) == len(x_list) == (n_groups if per_group else 1)
    offset_tiles = 0 if per_group else combined_row_offset // tm

    def combined(width, offset=offset_tiles):
        return pl.BlockSpec((tm, width), lambda i: (i + offset, 0))

    def grouped(width, g, block_offset=0):
        lo, n = starts[g], group_tiles[g]
        return pl.BlockSpec((tm, width), lambda i: (jnp.clip(i - lo, 0, n - 1) + block_offset, 0))

    in_specs, operands = [], []
    for g, parts in enumerate(t_sets):
        for t in parts:
            in_specs.append(grouped(t.shape[1], g) if per_group else combined(t.shape[1]))
            operands.append(t)
    for g, x in enumerate(x_list):
        in_specs.append(grouped(D_MODEL, g) if per_group else combined(D_MODEL))
        operands.append(x)
    for g, p in enumerate(p_list):
        in_specs.append(grouped(PLE_DIM, g, block_offset=layer * group_tiles[g]))
        operands.append(p)
    in_specs += [
        pl.BlockSpec((k, D_MODEL), lambda i: (0, 0), pipeline_mode=resident),
        pl.BlockSpec((PLE_DIM, D_MODEL), lambda i: (0, 0), pipeline_mode=resident),
        pl.BlockSpec((1, D_MODEL), lambda i: (0, 0)),
        pl.BlockSpec((D_MODEL, D_MODEL), lambda i: (0, 0), pipeline_mode=resident),
    ]
    operands += [w_out, w_proj, gain, w_gate]
    kernel = functools.partial(_out_pe_kernel, n_parts=n_parts, group_tiles=group_tiles,
                               per_group=per_group)
    return pl.pallas_call(
        kernel,
        name="out_pe",
        grid=(total_rows // tm,),
        in_specs=in_specs,
        out_specs=combined(D_MODEL, offset=0),
        out_shape=jax.ShapeDtypeStruct((total_rows, D_MODEL), F32),
        compiler_params=pltpu.CompilerParams(
            dimension_semantics=("arbitrary",),
            vmem_limit_bytes=VMEM_LIMIT_BYTES),
    )(*operands)


def _attn_front_kernel(x_ref, gain_ref, w_ref, cos_ref, sin_ref, qg_ref, kg_ref,
                       q_ref, k_ref, v_ref, zs_ref, hn_ref):
    tm = x_ref.shape[0]
    half = ROPE_DIM // 2
    tn = ATTN_FRONT_TN
    heads_per_block = tn // HEAD_DIM

    for r0 in range(0, tm, ATTN_FRONT_SUB):
        rows = slice(r0, r0 + ATTN_FRONT_SUB)
        x = x_ref[rows, :]
        hn_ref[rows, :] = (x * _rms_scale(x) * gain_ref[...]).astype(BF16)
        cos = cos_ref[:, rows].T
        sin = sin_ref[:, rows].T
        first_half = lax.broadcasted_iota(jnp.int32, cos.shape, 1) < half

        def norm_rope(slab, head_gain, cos=cos, sin=sin, first_half=first_half):
            s = slab * _rms_scale(slab) * head_gain
            partner = jnp.where(first_half, pltpu.roll(s, HEAD_DIM - half, axis=1),
                                pltpu.roll(s, half, axis=1))
            return s * cos + partner * sin

        for n in range(ATTN_IN_WIDTH // tn):
            y = _dot(hn_ref[rows, :], w_ref[:, n * tn:(n + 1) * tn])
            if n < Q_WIDTH // tn:
                for h in range(heads_per_block):
                    cols = slice(h * HEAD_DIM, (h + 1) * HEAD_DIM)
                    q_ref[rows, n * tn + h * HEAD_DIM:n * tn + (h + 1) * HEAD_DIM] = (
                        norm_rope(y[:, cols], qg_ref[...]).astype(BF16))
            elif n == Q_WIDTH // tn:
                for h in range(N_KV_HEADS):
                    cols = slice(h * HEAD_DIM, (h + 1) * HEAD_DIM)
                    k_ref[rows, cols] = norm_rope(y[:, cols], kg_ref[...]).astype(BF16)
                v_ref[rows, :] = y[:, KV_WIDTH:].astype(BF16)
            else:
                z0 = (n - Q_WIDTH // tn - 1) * tn
                zs_ref[rows, z0:z0 + tn] = _silu(y)


def _attn_front(x2d, gain, w_in, cos_t, sin_t, q_gain, k_gain, *, group_rows, group_seqs):
    rows = x2d.shape[0]
    tm = ATTN_FRONT_TM
    assert 2 * KV_WIDTH == ATTN_FRONT_TN

    def table_map(i):
        t0, _ = _tile_position(i * tm, group_rows, group_seqs)
        return (0, t0 // tm)

    table_spec = pl.BlockSpec((HEAD_DIM, tm), table_map)
    row_spec = lambda width: pl.BlockSpec((tm, width), lambda i: (i, 0))
    return pl.pallas_call(
        _attn_front_kernel,
        name="attn_front",
        grid=(rows // tm,),
        in_specs=[
            row_spec(D_MODEL),
            pl.BlockSpec((1, D_MODEL), lambda i: (0, 0)),
            pl.BlockSpec((D_MODEL, ATTN_IN_WIDTH), lambda i: (0, 0), pipeline_mode=pl.Buffered(1)),
            table_spec, table_spec,
            pl.BlockSpec((1, HEAD_DIM), lambda i: (0, 0)),
            pl.BlockSpec((1, HEAD_DIM), lambda i: (0, 0)),
        ],
        out_specs=[row_spec(Q_WIDTH), row_spec(KV_WIDTH), row_spec(KV_WIDTH), row_spec(Q_WIDTH)],
        out_shape=[
            jax.ShapeDtypeStruct((rows, Q_WIDTH), BF16),
            jax.ShapeDtypeStruct((rows, KV_WIDTH), BF16),
            jax.ShapeDtypeStruct((rows, KV_WIDTH), BF16),
            jax.ShapeDtypeStruct((rows, Q_WIDTH), F32),
        ],
        scratch_shapes=[pltpu.VMEM((tm, D_MODEL), BF16)],
        compiler_params=pltpu.CompilerParams(
            dimension_semantics=("arbitrary",),
            vmem_limit_bytes=VMEM_LIMIT_BYTES),
    )(x2d, gain, w_in, cos_t, sin_t, q_gain, k_gain)


def _attn_core_kernel(sink_ref, q_ref, kprev_ref, kmid_ref, knext_ref, vprev_ref, vmid_ref,
                      vnext_ref, zs_ref, o_ref, bias_ref, *, tm, group_rows, group_seqs):
    i = pl.program_id(0)
    blk = ATTN_BLOCK
    n_qb = tm // blk
    t0, seq = _tile_position(i * tm, group_rows, group_seqs)

    def window(prev_ref, mid_ref, next_ref, qb, cols):
        if qb == 0:
            return jnp.concatenate([prev_ref[:, cols], mid_ref[0:2 * blk, cols]], axis=0)
        if qb == n_qb - 1:
            return jnp.concatenate([mid_ref[tm - 2 * blk:tm, cols], next_ref[:, cols]], axis=0)
        return mid_ref[(qb - 1) * blk:(qb + 2) * blk, cols]

    ones = jnp.ones((3 * blk, HEAD_DIM), BF16)

    n_rows = GQA_GROUP * blk
    qi = lax.rem(lax.broadcasted_iota(jnp.int32, (n_rows, blk), 0), blk)
    kj = lax.broadcasted_iota(jnp.int32, (n_rows, blk), 1)
    prev_bias = jnp.where(kj >= qi, 0.0, NEG_BIG).astype(F32)
    next_bias = jnp.where(kj <= qi, 0.0, NEG_BIG).astype(F32)
    bias_ref[0] = jnp.where(t0 == 0, NEG_BIG, prev_bias)
    bias_ref[1] = prev_bias
    bias_ref[2] = jnp.where(t0 + tm == seq, NEG_BIG, next_bias)
    bias_ref[3] = next_bias

    sqrt_d = math.sqrt(HEAD_DIM)
    exp2_scale = LOG2_E / sqrt_d

    for qb in range(n_qb):
        row0 = qb * blk
        rows = pl.ds(row0, blk)
        b_prev = bias_ref[0 if qb == 0 else 1]
        b_next = bias_ref[2 if qb == n_qb - 1 else 3]
        for kvh in range(N_KV_HEADS):
            heads = [kvh * GQA_GROUP + hg for hg in range(GQA_GROUP)]
            qs = jnp.concatenate(
                [q_ref[rows, h * HEAD_DIM:(h + 1) * HEAD_DIM] for h in heads], axis=0)
            kv_cols = slice(kvh * HEAD_DIM, (kvh + 1) * HEAD_DIM)
            kw = window(kprev_ref, kmid_ref, knext_ref, qb, kv_cols)
            vw = jnp.concatenate([window(vprev_ref, vmid_ref, vnext_ref, qb, kv_cols), ones], axis=1)
            s = lax.dot_general(qs, kw, (((1,), (1,)), ((), ())), preferred_element_type=F32)
            sb = (s[:, 0:blk] + b_prev, s[:, blk:2 * blk], s[:, 2 * blk:] + b_next)
            row_max = jnp.max(jnp.maximum(jnp.maximum(sb[0], sb[1]), sb[2]), axis=-1, keepdims=True)
            e_rows, m_rows, sinks = [], [], []
            for hg, h in enumerate(heads):
                r = slice(hg * blk, (hg + 1) * blk)
                sink_raw = sink_ref[h] * sqrt_d
                m_h = jnp.maximum(row_max[r], sink_raw)
                e_rows.append(jnp.concatenate(
                    [jnp.exp2((b[r] - m_h) * exp2_scale).astype(BF16) for b in sb], axis=1))
                m_rows.append(m_h)
                sinks.append(sink_raw)
            e = jnp.concatenate(e_rows, axis=0)
            ov = _dot(e, vw)
            for hg, h in enumerate(heads):
                r = slice(hg * blk, (hg + 1) * blk)
                hcols = slice(h * HEAD_DIM, (h + 1) * HEAD_DIM)
                denom = ov[r, HEAD_DIM:] + jnp.exp2((sinks[hg] - m_rows[hg]) * exp2_scale)
                o_ref[rows, hcols] = (ov[r, :HEAD_DIM] * (1.0 / denom) * zs_ref[rows, hcols]
                                      ).astype(BF16)


def _attn_core(sink, q, k, v, zs, *, group_rows, group_seqs):
    rows = q.shape[0]
    tm = ATTN_CORE_TM
    blk = ATTN_BLOCK
    per = tm // blk
    n_blocks = rows // blk
    prev_map = lambda i: (jnp.maximum(i * per - 1, 0), 0)
    next_map = lambda i: (jnp.minimum((i + 1) * per, n_blocks - 1), 0)
    mid_map = lambda i: (i, 0)
    kernel = functools.partial(_attn_core_kernel, tm=tm, group_rows=group_rows,
                               group_seqs=group_seqs)
    return pl.pallas_call(
        kernel,
        name="attn_core",
        grid=(rows // tm,),
        in_specs=[
            pl.BlockSpec(memory_space=pltpu.SMEM),
            pl.BlockSpec((tm, Q_WIDTH), mid_map),
            pl.BlockSpec((blk, KV_WIDTH), prev_map),
            pl.BlockSpec((tm, KV_WIDTH), mid_map),
            pl.BlockSpec((blk, KV_WIDTH), next_map),
            pl.BlockSpec((blk, KV_WIDTH), prev_map),
            pl.BlockSpec((tm, KV_WIDTH), mid_map),
            pl.BlockSpec((blk, KV_WIDTH), next_map),
            pl.BlockSpec((tm, Q_WIDTH), mid_map),
        ],
        out_specs=pl.BlockSpec((tm, Q_WIDTH), mid_map),
        out_shape=jax.ShapeDtypeStruct((rows, Q_WIDTH), BF16),
        scratch_shapes=[pltpu.VMEM((4, GQA_GROUP * blk, blk), F32)],
        compiler_params=pltpu.CompilerParams(
            dimension_semantics=("arbitrary",),
            vmem_limit_bytes=VMEM_LIMIT_BYTES),
    )(sink, q, k, k, k, v, v, v, zs)


def _rope_tables(seq):
    freq = ROPE_THETA ** (-jnp.arange(0, ROPE_DIM, 2, dtype=F32) / ROPE_DIM)
    pos = jnp.arange(seq, dtype=F32)
    ang = freq[:, None] * pos[None, :]
    cos = jnp.cos(ang)
    sin = jnp.sin(ang)
    rest = HEAD_DIM - ROPE_DIM
    cos_t = jnp.concatenate([cos, cos, jnp.ones((rest, seq), F32)], axis=0)
    sin_t = jnp.concatenate([-sin, sin, jnp.zeros((rest, seq), F32)], axis=0)
    return cos_t, sin_t


def kernel(x_prompt, x_sample, p_prompt, p_sample, norm_g, a_w_in, a_w_grp, a_scale, a_w_out,
           b_w_in, b_q_norm, b_k_norm, b_sink, b_w_out, pe_w_proj, pe_norm_g, pe_w_gate):
    xs = (x_prompt, x_sample)
    group_seqs = tuple(x.shape[1] for x in xs)
    group_rows = tuple(x.shape[0] * x.shape[1] for x in xs)
    x2d = [x.reshape(rows, D_MODEL) for x, rows in zip(xs, group_rows)]
    p2d = [p.reshape(p.shape[0] * rows, PLE_DIM) for p, rows in zip((p_prompt, p_sample), group_rows)]

    a_w_ug, a_w_z = _fold_group_proj(a_w_in[0], a_w_grp[0])
    a_w_out_bf = a_w_out[0].astype(BF16)
    b_w_in_bf = b_w_in[0].astype(BF16)
    b_w_out_bf = b_w_out[0].astype(BF16)
    pe_w_proj_bf = pe_w_proj.astype(BF16)
    pe_w_gate_bf = pe_w_gate.astype(BF16)
    rope_cos, rope_sin = _rope_tables(max(group_seqs))

    t_sets = [[_pool_front(x, norm_g[0:1], a_w_ug, a_w_z, a_scale, seq=seq, part=part)
               for part in range(N_POOL_GROUPS // POOL_GROUPS_PER_CALL)]
              for x, seq in zip(x2d, group_seqs)]
    x1 = _out_pe(t_sets, a_w_out_bf, x2d, p2d, 0, pe_w_proj_bf[0], pe_norm_g[0:1], pe_w_gate_bf[0],
                 group_rows=group_rows)

    q, k, v, zs = _attn_front(x1, norm_g[1:2], b_w_in_bf, rope_cos, rope_sin, b_q_norm, b_k_norm,
                              group_rows=group_rows, group_seqs=group_seqs)
    og = _attn_core(b_sink[0], q, k, v, zs, group_rows=group_rows, group_seqs=group_seqs)
    ys = [_out_pe([[og]], b_w_out_bf, [x1], [p], 1, pe_w_proj_bf[1], pe_norm_g[1:2], pe_w_gate_bf[1],
                  group_rows=(rows,), combined_row_offset=sum(group_rows[:g]))
          for g, (p, rows) in enumerate(zip(p2d, group_rows))]
    return tuple(y.reshape(x.shape) for y, x in zip(ys, xs))
```

```python
import functools
import math

import jax
import jax.numpy as jnp
from jax import lax
from jax.experimental import pallas as pl
from jax.experimental.pallas import tpu as pltpu

F32 = jnp.float32
BF16 = jnp.bfloat16

D_MODEL = 2048
PLE_DIM = 256
EPS = 1e-6

POOL_WIDTH = 4096
POOL_WINDOWS = (2, 4, 8, 16)
N_POOL_GROUPS = len(POOL_WINDOWS)
POOL_GROUP_WIDTH = POOL_WIDTH // N_POOL_GROUPS
POOL_HALO = 8

HEAD_DIM = 128
N_Q_HEADS = 16
N_KV_HEADS = 4
GQA_GROUP = N_Q_HEADS // N_KV_HEADS
Q_WIDTH = N_Q_HEADS * HEAD_DIM
KV_WIDTH = N_KV_HEADS * HEAD_DIM
ATTN_IN_WIDTH = 2 * Q_WIDTH + 2 * KV_WIDTH
ATTN_BLOCK = 128
ROPE_DIM = HEAD_DIM // 4
ROPE_THETA = 500000.0
NEG_BIG = -1e30
LOG2_E = math.log2(math.e)

VMEM_LIMIT_BYTES = 56 * 1024 * 1024
LANES = 128
SUBLANES = 8

FOLD_ROW_BLOCK = 1024
POOL_TM = 512
POOL_GROUPS_PER_CALL = 2
POOL_ROW_BLOCK = 64
POOL_DIRECT_MAX = 8
ATTN_FRONT_TM = 512
ATTN_FRONT_TN = 1024
ATTN_FRONT_SUB = 256
ATTN_CORE_TM = 1024
OUT_TM_CHOICES = (512, 256)


def _rms_scale(x):
    return lax.rsqrt(jnp.mean(x * x, axis=-1, keepdims=True) + EPS)


def _dot(a, b):
    return jnp.dot(a, b, preferred_element_type=F32)


def _silu(z):
    return z * jax.nn.sigmoid(z)


def _tile_position(row0, group_rows, group_seqs):
    start = sum(group_rows[:-1])
    t0 = lax.rem(row0 - start, group_seqs[-1])
    seq = group_seqs[-1]
    for rows, s in zip(group_rows[-2::-1], group_seqs[-2::-1]):
        start -= rows
        inside = row0 < start + rows
        t0 = jnp.where(inside, lax.rem(row0 - start, s), t0)
        seq = jnp.where(inside, s, seq)
    return t0, seq


def _fold_group_kernel(wu_ref, wgrp_ref, wz_ref, wug_out_ref, wz_out_ref):
    wug_out_ref[...] = _dot(wu_ref[...].astype(BF16), wgrp_ref[0].astype(BF16)).astype(BF16)
    wz_out_ref[...] = wz_ref[...].astype(BF16)


def _fold_group_proj(w_in, w_grp):
    c = POOL_GROUP_WIDTH
    rows = FOLD_ROW_BLOCK
    out = jax.ShapeDtypeStruct((D_MODEL, POOL_WIDTH), BF16)
    return pl.pallas_call(
        _fold_group_kernel,
        name="fold_group_proj",
        grid=(N_POOL_GROUPS, D_MODEL // rows),
        in_specs=[
            pl.BlockSpec((rows, c), lambda g, r: (r, g)),
            pl.BlockSpec((1, c, c), lambda g, r: (g, 0, 0)),
            pl.BlockSpec((rows, c), lambda g, r: (r, N_POOL_GROUPS + g)),
        ],
        out_specs=[pl.BlockSpec((rows, c), lambda g, r: (r, g))] * 2,
        out_shape=[out, out],
        compiler_params=pltpu.CompilerParams(
            dimension_semantics=("arbitrary", "arbitrary"),
            vmem_limit_bytes=VMEM_LIMIT_BYTES),
    )(w_in, w_grp, w_in)


def _pool_front_kernel(x_ref, xprev_ref, xnext_ref, gain_ref, wug_ref, wz_ref,
                       scale_ref, t_ref, hn_ref, a_ref, zs_ref, h_ref, lvl_ref, *, tm, seq,
                       first_group, n_groups):
    i = pl.program_id(0)
    n_ext = tm + 2 * POOL_HALO
    c = POOL_GROUP_WIDTH
    n_levels = first_group + n_groups
    t0 = lax.rem(i * tm, seq)
    has_prev = t0 > 0
    has_next = t0 + tm < seq
    x = x_ref[...]
    r = _rms_scale(x)
    xp = xprev_ref[...]
    rp = jnp.where(has_prev, _rms_scale(xp), 0.0)
    xn = xnext_ref[...]
    rn = jnp.where(has_next, _rms_scale(xn), 0.0)
    edge = 2 * SUBLANES
    row = lax.broadcasted_iota(jnp.int32, (edge, 1), 0)
    inv_first, inv_last = [], []
    for win in POOL_WINDOWS[:n_levels]:
        half = win // 2
        for first_pos, out in ((t0 + row, inv_first), (t0 + tm - edge + row, inv_last)):
            cnt = jnp.minimum(first_pos + half, seq) - jnp.maximum(first_pos - half, 0)
            out.append(1.0 / cnt.astype(F32))

    pad = POOL_HALO
    top = pad + POOL_HALO
    n_buf = n_ext + 2 * pad
    n_chunks = D_MODEL // LANES
    for ch in range(n_chunks):
        cols = slice(ch * LANES, (ch + 1) * LANES)
        gain = gain_ref[:, cols]
        hc = x_ref[:, cols] * r * gain
        hn_ref[:, cols] = hc.astype(BF16)
        h_ref[top:top + tm, cols] = hc
        h_ref[pad:top, cols] = xprev_ref[:, cols] * rp * gain
        h_ref[top + tm:top + tm + POOL_HALO, cols] = xnext_ref[:, cols] * rn * gain
    h_ref[0:pad, :] = jnp.zeros((pad, D_MODEL), F32)
    h_ref[n_buf - pad:, :] = jnp.zeros((pad, D_MODEL), F32)
    for k in range(n_groups):
        gcols = slice(k * c, (k + 1) * c)
        zs_ref[:, gcols] = _silu(_dot(hn_ref[...], wz_ref[:, gcols]))

    centre_blocks = [(top + b, top + b + POOL_ROW_BLOCK) for b in range(0, tm, POOL_ROW_BLOCK)]
    all_blocks = [(pad, top)] + centre_blocks + [(top + tm, top + tm + POOL_HALO)]
    wins = POOL_WINDOWS[first_group:first_group + n_groups]
    n_sets = lvl_ref.shape[0]
    for ch in range(n_chunks):
        cols = slice(ch * LANES, (ch + 1) * LANES)
        lvl = lvl_ref.at[ch % n_sets]
        for k, win in enumerate(wins):
            gi = first_group + k
            feeds_next = k + 1 < n_groups and wins[k + 1] > POOL_DIRECT_MAX
            for b0, b1 in (all_blocks if feeds_next else centre_blocks):
                if win <= POOL_DIRECT_MAX:
                    terms = [h_ref[b0 + u:b1 + u, cols] for u in range(-(win // 2), win // 2)]
                    while len(terms) > 1:
                        terms = [terms[j] + terms[j + 1] for j in range(0, len(terms), 2)]
                    level = terms[0]
                else:
                    assert win == 2 * wins[k - 1] and wins[k - 1] <= POOL_DIRECT_MAX
                    level = lvl[b0 - win // 4:b1 - win // 4, :] + lvl[b0 + win // 4:b1 + win // 4, :]
                if feeds_next:
                    lvl[b0:b1, :] = level
                r0, r1 = b0 - top, b1 - top
                if r0 < 0 or r1 > tm:
                    continue
                h = h_ref[b0:b1, cols]
                inv = 1.0 / win
                if r0 == 0:
                    a_ref[k, 0:edge, cols] = (level[:edge] * inv_first[gi] - h[:edge]).astype(BF16)
                    a_ref[k, edge:r1, cols] = (level[edge:] * inv - h[edge:]).astype(BF16)
                elif r1 == tm:
                    a_ref[k, r0:tm - edge, cols] = (level[:-edge] * inv - h[:-edge]).astype(BF16)
                    a_ref[k, tm - edge:tm, cols] = (level[-edge:] * inv_last[gi] - h[-edge:]
                                                    ).astype(BF16)
                else:
                    a_ref[k, r0:r1, cols] = (level * inv - h).astype(BF16)

    for k in range(n_groups):
        gcols = slice(k * c, (k + 1) * c)
        mm = _dot(a_ref[k], wug_ref[:, gcols])
        t_ref[:, gcols] = ((mm * scale_ref[:, gcols]) * zs_ref[:, gcols]).astype(BF16)


def _pool_front(x2d, gain, w_ug, w_z, scale, *, seq, part):
    rows = x2d.shape[0]
    tm = POOL_TM
    n_groups = POOL_GROUPS_PER_CALL
    width = n_groups * POOL_GROUP_WIDTH
    n_row_blocks8 = rows // SUBLANES
    resident = pl.Buffered(1)
    kernel = functools.partial(_pool_front_kernel, tm=tm, seq=seq, first_group=part * n_groups,
                               n_groups=n_groups)
    return pl.pallas_call(
        kernel,
        name="pool_front",
        grid=(rows // tm,),
        in_specs=[
            pl.BlockSpec((tm, D_MODEL), lambda i: (i, 0)),
            pl.BlockSpec((POOL_HALO, D_MODEL),
                         lambda i: (jnp.maximum(i * (tm // SUBLANES) - 1, 0), 0)),
            pl.BlockSpec((POOL_HALO, D_MODEL),
                         lambda i: (jnp.minimum((i + 1) * (tm // SUBLANES), n_row_blocks8 - 1), 0)),
            pl.BlockSpec((1, D_MODEL), lambda i: (0, 0)),
            pl.BlockSpec((D_MODEL, width), lambda i: (0, part), pipeline_mode=resident),
            pl.BlockSpec((D_MODEL, width), lambda i: (0, part), pipeline_mode=resident),
            pl.BlockSpec((1, width), lambda i: (0, part)),
        ],
        out_specs=pl.BlockSpec((tm, width), lambda i: (i, 0)),
        out_shape=jax.ShapeDtypeStruct((rows, width), BF16),
        scratch_shapes=[
            pltpu.VMEM((tm, D_MODEL), BF16),
            pltpu.VMEM((n_groups, tm, D_MODEL), BF16),
            pltpu.VMEM((tm, width), F32),
            pltpu.VMEM((tm + 4 * POOL_HALO, D_MODEL), F32),
            pltpu.VMEM((2, tm + 4 * POOL_HALO, LANES), F32),
        ],
        compiler_params=pltpu.CompilerParams(
            dimension_semantics=("arbitrary",),
            vmem_limit_bytes=VMEM_LIMIT_BYTES),
    )(x2d, x2d, x2d, gain, w_ug, w_z, scale)


def _out_pe_math(t_refs, wout_ref, x_ref, p_ref, wp_ref, gain_ref, wgate_ref, out_ref):
    y = None
    k0 = 0
    for t_ref in t_refs:
        k1 = k0 + t_ref.shape[1]
        part = _dot(t_ref[...], wout_ref[k0:k1, :])
        y = part if y is None else y + part
        k0 = k1
    h = x_ref[...] + y
    hn = (h * _rms_scale(h) * gain_ref[...]).astype(BF16)
    gate = jax.nn.sigmoid(_dot(hn, wgate_ref[...]))
    e = _dot(p_ref[...].astype(BF16), wp_ref[...])
    out_ref[...] = h + e * gate


def _out_pe_kernel(*refs, n_parts, group_tiles, per_group):
    n_groups = len(group_tiles)
    n_sources = n_groups if per_group else 1
    refs = list(refs)

    def take(count):
        taken = refs[:count]
        del refs[:count]
        return taken

    t_sets = [take(n_parts) for _ in range(n_sources)]
    x_refs = take(n_sources)
    p_refs = take(n_groups)
    wout_ref, wp_ref, gain_ref, wgate_ref, out_ref = take(5)
    assert not refs

    i = pl.program_id(0)
    start = 0
    for g, tiles in enumerate(group_tiles):
        src = g if per_group else 0
        body = functools.partial(_out_pe_math, t_sets[src], wout_ref, x_refs[src], p_refs[g], wp_ref,
                                 gain_ref, wgate_ref, out_ref)
        pl.when((i >= start) & (i < start + tiles))(body)
        start += tiles


def _out_pe_tile(k, n_t, n_x, n_p):
    weights = 2 * (k * D_MODEL + PLE_DIM * D_MODEL + D_MODEL * D_MODEL)
    for tm in OUT_TM_CHOICES:
        row_blocks = 2 * tm * (n_t * 2 * k + n_x * 4 * D_MODEL + n_p * 4 * PLE_DIM + 4 * D_MODEL)
        temporaries = 2 * tm * D_MODEL * 4
        if weights + row_blocks + temporaries <= VMEM_LIMIT_BYTES:
            return tm
    raise ValueError(f"no out_pe row tile fits VMEM for k={k}")


def _out_pe(t_sets, w_out, x_list, p_list, layer, w_proj, gain, w_gate, *, group_rows,
            combined_row_offset=None):
    n_groups = len(group_rows)
    n_parts = len(t_sets[0])
    k = sum(t.shape[1] for t in t_sets[0])
    assert k == w_out.shape[0] and len(p_list) == n_groups
    tm = _out_pe_tile(k, len(t_sets), len(x_list), n_groups)
    group_tiles = tuple(rows // tm for rows in group_rows)
    starts = tuple(sum(group_tiles[:g]) for g in range(n_groups))
    total_rows = sum(group_rows)
    resident = pl.Buffered(1)

    per_group = combined_row_offset is None
    assert len(t_sets) == len(x_list) == (n_groups if per_group else 1)
    offset_tiles = 0 if per_group else combined_row_offset // tm

    def combined(width, offset=offset_tiles):
        return pl.BlockSpec((tm, width), lambda i: (i + offset, 0))

    def grouped(width, g, block_offset=0):
        lo, n = starts[g], group_tiles[g]
        return pl.BlockSpec((tm, width), lambda i: (jnp.clip(i - lo, 0, n - 1) + block_offset, 0))

    in_specs, operands = [], []
    for g, parts in enumerate(t_sets):
        for t in parts:
            in_specs.append(grouped(t.shape[1], g) if per_group else combined(t.shape[1]))
            operands.append(t)
    for g, x in enumerate(x_list):
        in_specs.append(grouped(D_MODEL, g) if per_group else combined(D_MODEL))
        operands.append(x)
    for g, p in enumerate(p_list):
        in_specs.append(grouped(PLE_DIM, g, block_offset=layer * group_tiles[g]))
        operands.append(p)
    in_specs += [
        pl.BlockSpec((k, D_MODEL), lambda i: (0, 0), pipeline_mode=resident),
        pl.BlockSpec((PLE_DIM, D_MODEL), lambda i: (0, 0), pipeline_mode=resident),
        pl.BlockSpec((1, D_MODEL), lambda i: (0, 0)),
        pl.BlockSpec((D_MODEL, D_MODEL), lambda i: (0, 0), pipeline_mode=resident),
    ]
    operands += [w_out, w_proj, gain, w_gate]
    kernel = functools.partial(_out_pe_kernel, n_parts=n_parts, group_tiles=group_tiles,
                               per_group=per_group)
    return pl.pallas_call(
        kernel,
        name="out_pe",
        grid=(total_rows // tm,),
        in_specs=in_specs,
        out_specs=combined(D_MODEL, offset=0),
        out_shape=jax.ShapeDtypeStruct((total_rows, D_MODEL), F32),
        compiler_params=pltpu.CompilerParams(
            dimension_semantics=("arbitrary",),
            vmem_limit_bytes=VMEM_LIMIT_BYTES),
    )(*operands)


def _attn_front_kernel(x_ref, gain_ref, w_ref, cos_ref, sin_ref, qg_ref, kg_ref,
                       q_ref, k_ref, v_ref, zs_ref, hn_ref):
    tm = x_ref.shape[0]
    half = ROPE_DIM // 2
    tn = ATTN_FRONT_TN
    heads_per_block = tn // HEAD_DIM

    for r0 in range(0, tm, ATTN_FRONT_SUB):
        rows = slice(r0, r0 + ATTN_FRONT_SUB)
        x = x_ref[rows, :]
        hn_ref[rows, :] = (x * _rms_scale(x) * gain_ref[...]).astype(BF16)
        cos = cos_ref[:, rows].T
        sin = sin_ref[:, rows].T
        first_half = lax.broadcasted_iota(jnp.int32, cos.shape, 1) < half

        def norm_rope(slab, head_gain, cos=cos, sin=sin, first_half=first_half):
            s = slab * _rms_scale(slab) * head_gain
            partner = jnp.where(first_half, pltpu.roll(s, HEAD_DIM - half, axis=1),
                                pltpu.roll(s, half, axis=1))
            return s * cos + partner * sin

        for n in range(ATTN_IN_WIDTH // tn):
            y = _dot(hn_ref[rows, :], w_ref[:, n * tn:(n + 1) * tn])
            if n < Q_WIDTH // tn:
                for h in range(heads_per_block):
                    cols = slice(h * HEAD_DIM, (h + 1) * HEAD_DIM)
                    q_ref[rows, n * tn + h * HEAD_DIM:n * tn + (h + 1) * HEAD_DIM] = (
                        norm_rope(y[:, cols], qg_ref[...]).astype(BF16))
            elif n == Q_WIDTH // tn:
                for h in range(N_KV_HEADS):
                    cols = slice(h * HEAD_DIM, (h + 1) * HEAD_DIM)
                    k_ref[rows, cols] = norm_rope(y[:, cols], kg_ref[...]).astype(BF16)
                v_ref[rows, :] = y[:, KV_WIDTH:].astype(BF16)
            else:
                z0 = (n - Q_WIDTH // tn - 1) * tn
                zs_ref[rows, z0:z0 + tn] = _silu(y)


def _attn_front(x2d, gain, w_in, cos_t, sin_t, q_gain, k_gain, *, group_rows, group_seqs):
    rows = x2d.shape[0]
    tm = ATTN_FRONT_TM
    assert 2 * KV_WIDTH == ATTN_FRONT_TN

    def table_map(i):
        t0, _ = _tile_position(i * tm, group_rows, group_seqs)
        return (0, t0 // tm)

    table_spec = pl.BlockSpec((HEAD_DIM, tm), table_map)
    row_spec = lambda width: pl.BlockSpec((tm, width), lambda i: (i, 0))
    return pl.pallas_call(
        _attn_front_kernel,
        name="attn_front",
        grid=(rows // tm,),
        in_specs=[
            row_spec(D_MODEL),
            pl.BlockSpec((1, D_MODEL), lambda i: (0, 0)),
            pl.BlockSpec((D_MODEL, ATTN_IN_WIDTH), lambda i: (0, 0), pipeline_mode=pl.Buffered(1)),
            table_spec, table_spec,
            pl.BlockSpec((1, HEAD_DIM), lambda i: (0, 0)),
            pl.BlockSpec((1, HEAD_DIM), lambda i: (0, 0)),
        ],
        out_specs=[row_spec(Q_WIDTH), row_spec(KV_WIDTH), row_spec(KV_WIDTH), row_spec(Q_WIDTH)],
        out_shape=[
            jax.ShapeDtypeStruct((rows, Q_WIDTH), BF16),
            jax.ShapeDtypeStruct((rows, KV_WIDTH), BF16),
            jax.ShapeDtypeStruct((rows, KV_WIDTH), BF16),
            jax.ShapeDtypeStruct((rows, Q_WIDTH), F32),
        ],
        scratch_shapes=[pltpu.VMEM((tm, D_MODEL), BF16)],
        compiler_params=pltpu.CompilerParams(
            dimension_semantics=("arbitrary",),
            vmem_limit_bytes=VMEM_LIMIT_BYTES),
    )(x2d, gain, w_in, cos_t, sin_t, q_gain, k_gain)


def _attn_core_kernel(sink_ref, band_ref, q_ref, kprev_ref, kmid_ref, knext_ref, vprev_ref, vmid_ref,
                      vnext_ref, zs_ref, o_ref, *, tm, group_rows, group_seqs):
    i = pl.program_id(0)
    blk = ATTN_BLOCK
    n_qb = tm // blk
    t0, seq = _tile_position(i * tm, group_rows, group_seqs)

    def window(prev_ref, mid_ref, next_ref, qb, cols):
        if qb == 0:
            return jnp.concatenate([prev_ref[:, cols], mid_ref[0:2 * blk, cols]], axis=0)
        if qb == n_qb - 1:
            return jnp.concatenate([mid_ref[tm - 2 * blk:tm, cols], next_ref[:, cols]], axis=0)
        return mid_ref[(qb - 1) * blk:(qb + 2) * blk, cols]

    ones = jnp.ones((3 * blk, HEAD_DIM), BF16)


    sqrt_d = math.sqrt(HEAD_DIM)
    exp2_scale = LOG2_E / sqrt_d

    for qb in range(n_qb):
        row0 = qb * blk
        rows = pl.ds(row0, blk)
        b_prev = band_ref[0]
        b_next = band_ref[1]
        if qb == 0:
            b_prev = jnp.where(t0 == 0, NEG_BIG, b_prev)
        if qb == n_qb - 1:
            b_next = jnp.where(t0 + tm == seq, NEG_BIG, b_next)
        for kvh in range(N_KV_HEADS):
            heads = [kvh * GQA_GROUP + hg for hg in range(GQA_GROUP)]
            qs = jnp.concatenate(
                [q_ref[rows, h * HEAD_DIM:(h + 1) * HEAD_DIM] for h in heads], axis=0)
            kv_cols = slice(kvh * HEAD_DIM, (kvh + 1) * HEAD_DIM)
            kw = window(kprev_ref, kmid_ref, knext_ref, qb, kv_cols)
            vw = jnp.concatenate([window(vprev_ref, vmid_ref, vnext_ref, qb, kv_cols), ones], axis=1)
            s = lax.dot_general(qs, kw, (((1,), (1,)), ((), ())), preferred_element_type=F32)
            sb = (s[:, 0:blk] + b_prev, s[:, blk:2 * blk], s[:, 2 * blk:] + b_next)
            row_max = jnp.max(jnp.maximum(jnp.maximum(sb[0], sb[1]), sb[2]), axis=-1, keepdims=True)
            e_rows, m_rows, sinks = [], [], []
            for hg, h in enumerate(heads):
                r = slice(hg * blk, (hg + 1) * blk)
                sink_raw = sink_ref[h] * sqrt_d
                m_h = jnp.maximum(row_max[r], sink_raw)
                e_rows.append(jnp.concatenate(
                    [jnp.exp2((b[r] - m_h) * exp2_scale).astype(BF16) for b in sb], axis=1))
                m_rows.append(m_h)
                sinks.append(sink_raw)
            e = jnp.concatenate(e_rows, axis=0)
            ov = _dot(e, vw)
            for hg, h in enumerate(heads):
                r = slice(hg * blk, (hg + 1) * blk)
                hcols = slice(h * HEAD_DIM, (h + 1) * HEAD_DIM)
                denom = ov[r, HEAD_DIM:] + jnp.exp2((sinks[hg] - m_rows[hg]) * exp2_scale)
                o_ref[rows, hcols] = (ov[r, :HEAD_DIM] * (1.0 / denom) * zs_ref[rows, hcols]
                                      ).astype(BF16)


def _attn_core(sink, q, k, v, zs, *, group_rows, group_seqs):
    rows = q.shape[0]
    tm = ATTN_CORE_TM
    blk = ATTN_BLOCK
    per = tm // blk
    n_blocks = rows // blk
    prev_map = lambda i: (jnp.maximum(i * per - 1, 0), 0)
    next_map = lambda i: (jnp.minimum((i + 1) * per, n_blocks - 1), 0)
    mid_map = lambda i: (i, 0)
    n_rows = GQA_GROUP * blk
    qi = lax.rem(lax.broadcasted_iota(jnp.int32, (n_rows, blk), 0), blk)
    kj = lax.broadcasted_iota(jnp.int32, (n_rows, blk), 1)
    band = jnp.stack([jnp.where(kj >= qi, 0.0, NEG_BIG), jnp.where(kj <= qi, 0.0, NEG_BIG)]).astype(F32)
    kernel = functools.partial(_attn_core_kernel, tm=tm, group_rows=group_rows,
                               group_seqs=group_seqs)
    return pl.pallas_call(
        kernel,
        name="attn_core",
        grid=(rows // tm,),
        in_specs=[
            pl.BlockSpec(memory_space=pltpu.SMEM),
            pl.BlockSpec((2, n_rows, blk), lambda i: (0, 0, 0), pipeline_mode=pl.Buffered(1)),
            pl.BlockSpec((tm, Q_WIDTH), mid_map),
            pl.BlockSpec((blk, KV_WIDTH), prev_map),
            pl.BlockSpec((tm, KV_WIDTH), mid_map),
            pl.BlockSpec((blk, KV_WIDTH), next_map),
            pl.BlockSpec((blk, KV_WIDTH), prev_map),
            pl.BlockSpec((tm, KV_WIDTH), mid_map),
            pl.BlockSpec((blk, KV_WIDTH), next_map),
            pl.BlockSpec((tm, Q_WIDTH), mid_map),
        ],
        out_specs=pl.BlockSpec((tm, Q_WIDTH), mid_map),
        out_shape=jax.ShapeDtypeStruct((rows, Q_WIDTH), BF16),
        compiler_params=pltpu.CompilerParams(
            dimension_semantics=("arbitrary",),
            vmem_limit_bytes=VMEM_LIMIT_BYTES),
    )(sink, band, q, k, k, k, v, v, v, zs)


def _rope_tables(seq):
    freq = ROPE_THETA ** (-jnp.arange(0, ROPE_DIM, 2, dtype=F32) / ROPE_DIM)
    pos = jnp.arange(seq, dtype=F32)
    ang = freq[:, None] * pos[None, :]
    cos = jnp.cos(ang)
    sin = jnp.sin(ang)
    rest = HEAD_DIM - ROPE_DIM
    cos_t = jnp.concatenate([cos, cos, jnp.ones((rest, seq), F32)], axis=0)
    sin_t = jnp.concatenate([-sin, sin, jnp.zeros((rest, seq), F32)], axis=0)
    return cos_t, sin_t


def kernel(x_prompt, x_sample, p_prompt, p_sample, norm_g, a_w_in, a_w_grp, a_scale, a_w_out,
           b_w_in, b_q_norm, b_k_norm, b_sink, b_w_out, pe_w_proj, pe_norm_g, pe_w_gate):
    xs = (x_prompt, x_sample)
    group_seqs = tuple(x.shape[1] for x in xs)
    group_rows = tuple(x.shape[0] * x.shape[1] for x in xs)
    x2d = [x.reshape(rows, D_MODEL) for x, rows in zip(xs, group_rows)]
    p2d = [p.reshape(p.shape[0] * rows, PLE_DIM) for p, rows in zip((p_prompt, p_sample), group_rows)]

    a_w_ug, a_w_z = _fold_group_proj(a_w_in[0], a_w_grp[0])
    a_w_out_bf = a_w_out[0].astype(BF16)
    b_w_in_bf = b_w_in[0].astype(BF16)
    b_w_out_bf = b_w_out[0].astype(BF16)
    pe_w_proj_bf = pe_w_proj.astype(BF16)
    pe_w_gate_bf = pe_w_gate.astype(BF16)
    rope_cos, rope_sin = _rope_tables(max(group_seqs))

    t_sets = [[_pool_front(x, norm_g[0:1], a_w_ug, a_w_z, a_scale, seq=seq, part=part)
               for part in range(N_POOL_GROUPS // POOL_GROUPS_PER_CALL)]
              for x, seq in zip(x2d, group_seqs)]
    x1 = _out_pe(t_sets, a_w_out_bf, x2d, p2d, 0, pe_w_proj_bf[0], pe_norm_g[0:1], pe_w_gate_bf[0],
                 group_rows=group_rows)

    q, k, v, zs = _attn_front(x1, norm_g[1:2], b_w_in_bf, rope_cos, rope_sin, b_q_norm, b_k_norm,
                              group_rows=group_rows, group_seqs=group_seqs)
    og = _attn_core(b_sink[0], q, k, v, zs, group_rows=group_rows, group_seqs=group_seqs)
    ys = [_out_pe([[og]], b_w_out_bf, [x1], [p], 1, pe_w_proj_bf[1], pe_norm_g[1:2], pe_w_gate_bf[1],
                  group_rows=(rows,), combined_row_offset=sum(group_rows[:g]))
          for g, (p, rows) in enumerate(zip(p2d, group_rows))]
    return tuple(y.reshape(x.shape) for y, x in zip(ys, xs))
```

```python
import functools
import math

import jax
import jax.numpy as jnp
from jax import lax
from jax.experimental import pallas as pl
from jax.experimental.pallas import tpu as pltpu

F32 = jnp.float32
BF16 = jnp.bfloat16

D_MODEL = 2048
PLE_DIM = 256
EPS = 1e-6

POOL_WIDTH = 4096
POOL_WINDOWS = (2, 4, 8, 16)
N_POOL_GROUPS = len(POOL_WINDOWS)
POOL_GROUP_WIDTH = POOL_WIDTH // N_POOL_GROUPS
POOL_HALO = 8

HEAD_DIM = 128
N_Q_HEADS = 16
N_KV_HEADS = 4
GQA_GROUP = N_Q_HEADS // N_KV_HEADS
Q_WIDTH = N_Q_HEADS * HEAD_DIM
KV_WIDTH = N_KV_HEADS * HEAD_DIM
ATTN_IN_WIDTH = 2 * Q_WIDTH + 2 * KV_WIDTH
ATTN_BLOCK = 128
ROPE_DIM = HEAD_DIM // 4
ROPE_THETA = 500000.0
NEG_BIG = -1e30
LOG2_E = math.log2(math.e)

VMEM_LIMIT_BYTES = 56 * 1024 * 1024
LANES = 128
SUBLANES = 8

FOLD_ROW_BLOCK = 1024
POOL_TM = 512
POOL_GROUPS_PER_CALL = 2
POOL_SUB = 256
POOL_ROW_BLOCK = 64
POOL_DIRECT_MAX = 8
ATTN_FRONT_TM = 512
ATTN_FRONT_TN = 1024
ATTN_FRONT_SUB = 256
ATTN_CORE_TM = 1024
OUT_TM_CHOICES = (512, 256)


def _rms_scale(x):
    return lax.rsqrt(jnp.mean(x * x, axis=-1, keepdims=True) + EPS)


def _dot(a, b):
    return jnp.dot(a, b, preferred_element_type=F32)


def _silu(z):
    return z * jax.nn.sigmoid(z)


def _tile_position(row0, group_rows, group_seqs):
    start = sum(group_rows[:-1])
    t0 = lax.rem(row0 - start, group_seqs[-1])
    seq = group_seqs[-1]
    for rows, s in zip(group_rows[-2::-1], group_seqs[-2::-1]):
        start -= rows
        inside = row0 < start + rows
        t0 = jnp.where(inside, lax.rem(row0 - start, s), t0)
        seq = jnp.where(inside, s, seq)
    return t0, seq


def _fold_group_kernel(wu_ref, wgrp_ref, wz_ref, wug_out_ref, wz_out_ref):
    wug_out_ref[...] = _dot(wu_ref[...].astype(BF16), wgrp_ref[0].astype(BF16)).astype(BF16)
    wz_out_ref[...] = wz_ref[...].astype(BF16)


def _fold_group_proj(w_in, w_grp):
    c = POOL_GROUP_WIDTH
    rows = FOLD_ROW_BLOCK
    out = jax.ShapeDtypeStruct((D_MODEL, POOL_WIDTH), BF16)
    return pl.pallas_call(
        _fold_group_kernel,
        name="fold_group_proj",
        grid=(N_POOL_GROUPS, D_MODEL // rows),
        in_specs=[
            pl.BlockSpec((rows, c), lambda g, r: (r, g)),
            pl.BlockSpec((1, c, c), lambda g, r: (g, 0, 0)),
            pl.BlockSpec((rows, c), lambda g, r: (r, N_POOL_GROUPS + g)),
        ],
        out_specs=[pl.BlockSpec((rows, c), lambda g, r: (r, g))] * 2,
        out_shape=[out, out],
        compiler_params=pltpu.CompilerParams(
            dimension_semantics=("arbitrary", "arbitrary"),
            vmem_limit_bytes=VMEM_LIMIT_BYTES),
    )(w_in, w_grp, w_in)


def _pool_front_kernel(x_ref, xprev_ref, xnext_ref, gain_ref, wug_ref, wz_ref,
                       scale_ref, t_ref, hn_ref, a_ref, zs_ref, h_ref, lvl_ref, *, tm, seq,
                       first_group, n_groups):
    i = pl.program_id(0)
    n_ext = tm + 2 * POOL_HALO
    c = POOL_GROUP_WIDTH
    n_levels = first_group + n_groups
    t0 = lax.rem(i * tm, seq)
    has_prev = t0 > 0
    has_next = t0 + tm < seq
    x = x_ref[...]
    r = _rms_scale(x)
    xp = xprev_ref[...]
    rp = jnp.where(has_prev, _rms_scale(xp), 0.0)
    xn = xnext_ref[...]
    rn = jnp.where(has_next, _rms_scale(xn), 0.0)
    edge = 2 * SUBLANES
    row = lax.broadcasted_iota(jnp.int32, (edge, 1), 0)
    inv_first, inv_last = [], []
    for win in POOL_WINDOWS[:n_levels]:
        half = win // 2
        for first_pos, out in ((t0 + row, inv_first), (t0 + tm - edge + row, inv_last)):
            cnt = jnp.minimum(first_pos + half, seq) - jnp.maximum(first_pos - half, 0)
            out.append(1.0 / cnt.astype(F32))

    pad = POOL_HALO
    top = pad + POOL_HALO
    n_buf = n_ext + 2 * pad
    n_chunks = D_MODEL // LANES
    for s0 in range(0, tm, POOL_SUB):
        rows = slice(s0, s0 + POOL_SUB)
        rs = r[rows]
        for ch in range(n_chunks):
            cols = slice(ch * LANES, (ch + 1) * LANES)
            hc = x_ref[rows, cols] * rs * gain_ref[:, cols]
            hn_ref[rows, cols] = hc.astype(BF16)
            h_ref[top + s0:top + s0 + POOL_SUB, cols] = hc
        for k in range(n_groups):
            gcols = slice(k * c, (k + 1) * c)
            zs_ref[rows, gcols] = _silu(_dot(hn_ref[rows, :], wz_ref[:, gcols]))
    for ch in range(n_chunks):
        cols = slice(ch * LANES, (ch + 1) * LANES)
        gain = gain_ref[:, cols]
        h_ref[pad:top, cols] = xprev_ref[:, cols] * rp * gain
        h_ref[top + tm:top + tm + POOL_HALO, cols] = xnext_ref[:, cols] * rn * gain
    h_ref[0:pad, :] = jnp.zeros((pad, D_MODEL), F32)
    h_ref[n_buf - pad:, :] = jnp.zeros((pad, D_MODEL), F32)

    centre_blocks = [(top + b, top + b + POOL_ROW_BLOCK) for b in range(0, tm, POOL_ROW_BLOCK)]
    all_blocks = [(pad, top)] + centre_blocks + [(top + tm, top + tm + POOL_HALO)]
    wins = POOL_WINDOWS[first_group:first_group + n_groups]
    n_sets = lvl_ref.shape[0]
    for ch in range(n_chunks):
        cols = slice(ch * LANES, (ch + 1) * LANES)
        lvl = lvl_ref.at[ch % n_sets]
        for k, win in enumerate(wins):
            gi = first_group + k
            feeds_next = k + 1 < n_groups and wins[k + 1] > POOL_DIRECT_MAX
            for b0, b1 in (all_blocks if feeds_next else centre_blocks):
                if win <= POOL_DIRECT_MAX:
                    terms = [h_ref[b0 + u:b1 + u, cols] for u in range(-(win // 2), win // 2)]
                    while len(terms) > 1:
                        terms = [terms[j] + terms[j + 1] for j in range(0, len(terms), 2)]
                    level = terms[0]
                else:
                    assert win == 2 * wins[k - 1] and wins[k - 1] <= POOL_DIRECT_MAX
                    level = lvl[b0 - win // 4:b1 - win // 4, :] + lvl[b0 + win // 4:b1 + win // 4, :]
                if feeds_next:
                    lvl[b0:b1, :] = level
                r0, r1 = b0 - top, b1 - top
                if r0 < 0 or r1 > tm:
                    continue
                h = h_ref[b0:b1, cols]
                inv = 1.0 / win
                if r0 == 0:
                    a_ref[k, 0:edge, cols] = (level[:edge] * inv_first[gi] - h[:edge]).astype(BF16)
                    a_ref[k, edge:r1, cols] = (level[edge:] * inv - h[edge:]).astype(BF16)
                elif r1 == tm:
                    a_ref[k, r0:tm - edge, cols] = (level[:-edge] * inv - h[:-edge]).astype(BF16)
                    a_ref[k, tm - edge:tm, cols] = (level[-edge:] * inv_last[gi] - h[-edge:]
                                                    ).astype(BF16)
                else:
                    a_ref[k, r0:r1, cols] = (level * inv - h).astype(BF16)

    for k in range(n_groups):
        gcols = slice(k * c, (k + 1) * c)
        mm = _dot(a_ref[k], wug_ref[:, gcols])
        t_ref[:, gcols] = ((mm * scale_ref[:, gcols]) * zs_ref[:, gcols]).astype(BF16)


def _pool_front(x2d, gain, w_ug, w_z, scale, *, seq, part):
    rows = x2d.shape[0]
    tm = POOL_TM
    n_groups = POOL_GROUPS_PER_CALL
    width = n_groups * POOL_GROUP_WIDTH
    n_row_blocks8 = rows // SUBLANES
    resident = pl.Buffered(1)
    kernel = functools.partial(_pool_front_kernel, tm=tm, seq=seq, first_group=part * n_groups,
                               n_groups=n_groups)
    return pl.pallas_call(
        kernel,
        name="pool_front",
        grid=(rows // tm,),
        in_specs=[
            pl.BlockSpec((tm, D_MODEL), lambda i: (i, 0)),
            pl.BlockSpec((POOL_HALO, D_MODEL),
                         lambda i: (jnp.maximum(i * (tm // SUBLANES) - 1, 0), 0)),
            pl.BlockSpec((POOL_HALO, D_MODEL),
                         lambda i: (jnp.minimum((i + 1) * (tm // SUBLANES), n_row_blocks8 - 1), 0)),
            pl.BlockSpec((1, D_MODEL), lambda i: (0, 0)),
            pl.BlockSpec((D_MODEL, width), lambda i: (0, part), pipeline_mode=resident),
            pl.BlockSpec((D_MODEL, width), lambda i: (0, part), pipeline_mode=resident),
            pl.BlockSpec((1, width), lambda i: (0, part)),
        ],
        out_specs=pl.BlockSpec((tm, width), lambda i: (i, 0)),
        out_shape=jax.ShapeDtypeStruct((rows, width), BF16),
        scratch_shapes=[
            pltpu.VMEM((tm, D_MODEL), BF16),
            pltpu.VMEM((n_groups, tm, D_MODEL), BF16),
            pltpu.VMEM((tm, width), F32),
            pltpu.VMEM((tm + 4 * POOL_HALO, D_MODEL), F32),
            pltpu.VMEM((2, tm + 4 * POOL_HALO, LANES), F32),
        ],
        compiler_params=pltpu.CompilerParams(
            dimension_semantics=("arbitrary",),
            vmem_limit_bytes=VMEM_LIMIT_BYTES),
    )(x2d, x2d, x2d, gain, w_ug, w_z, scale)


def _out_pe_math(t_refs, wout_ref, x_ref, p_ref, wp_ref, gain_ref, wgate_ref, out_ref):
    y = None
    k0 = 0
    for t_ref in t_refs:
        k1 = k0 + t_ref.shape[1]
        part = _dot(t_ref[...], wout_ref[k0:k1, :])
        y = part if y is None else y + part
        k0 = k1
    h = x_ref[...] + y
    hn = (h * _rms_scale(h) * gain_ref[...]).astype(BF16)
    gate_pre = _dot(hn, wgate_ref[...])
    e = _dot(p_ref[...].astype(BF16), wp_ref[...])
    half = D_MODEL // 2
    for c0 in (0, half):
        cols = slice(c0, c0 + half)
        out_ref[:, cols] = h[:, cols] + e[:, cols] * jax.nn.sigmoid(gate_pre[:, cols])


def _out_pe_kernel(*refs, n_parts, group_tiles, per_group):
    n_groups = len(group_tiles)
    n_sources = n_groups if per_group else 1
    refs = list(refs)

    def take(count):
        taken = refs[:count]
        del refs[:count]
        return taken

    t_sets = [take(n_parts) for _ in range(n_sources)]
    x_refs = take(n_sources)
    p_refs = take(n_groups)
    wout_ref, wp_ref, gain_ref, wgate_ref, out_ref = take(5)
    assert not refs

    i = pl.program_id(0)
    start = 0
    for g, tiles in enumerate(group_tiles):
        src = g if per_group else 0
        body = functools.partial(_out_pe_math, t_sets[src], wout_ref, x_refs[src], p_refs[g], wp_ref,
                                 gain_ref, wgate_ref, out_ref)
        pl.when((i >= start) & (i < start + tiles))(body)
        start += tiles


def _out_pe_tile(k, n_t, n_x, n_p):
    weights = 2 * (k * D_MODEL + PLE_DIM * D_MODEL + D_MODEL * D_MODEL)
    for tm in OUT_TM_CHOICES:
        row_blocks = 2 * tm * (n_t * 2 * k + n_x * 4 * D_MODEL + n_p * 4 * PLE_DIM + 4 * D_MODEL)
        temporaries = 2 * tm * D_MODEL * 4
        if weights + row_blocks + temporaries <= VMEM_LIMIT_BYTES:
            return tm
    raise ValueError(f"no out_pe row tile fits VMEM for k={k}")


def _out_pe(t_sets, w_out, x_list, p_list, layer, w_proj, gain, w_gate, *, group_rows,
            combined_row_offset=None):
    n_groups = len(group_rows)
    n_parts = len(t_sets[0])
    k = sum(t.shape[1] for t in t_sets[0])
    assert k == w_out.shape[0] and len(p_list) == n_groups
    tm = _out_pe_tile(k, len(t_sets), len(x_list), n_groups)
    group_tiles = tuple(rows // tm for rows in group_rows)
    starts = tuple(sum(group_tiles[:g]) for g in range(n_groups))
    total_rows = sum(group_rows)
    resident = pl.Buffered(1)

    per_group = combined_row_offset is None
    assert len(t_sets) == len(x_list) == (n_groups if per_group else 1)
    offset_tiles = 0 if per_group else combined_row_offset // tm

    def combined(width, offset=offset_tiles):
        return pl.BlockSpec((tm, width), lambda i: (i + offset, 0))

    def grouped(width, g, block_offset=0):
        lo, n = starts[g], group_tiles[g]
        return pl.BlockSpec((tm, width), lambda i: (jnp.clip(i - lo, 0, n - 1) + block_offset, 0))

    in_specs, operands = [], []
    for g, parts in enumerate(t_sets):
        for t in parts:
            in_specs.append(grouped(t.shape[1], g) if per_group else combined(t.shape[1]))
            operands.append(t)
    for g, x in enumerate(x_list):
        in_specs.append(grouped(D_MODEL, g) if per_group else combined(D_MODEL))
        operands.append(x)
    for g, p in enumerate(p_list):
        in_specs.append(grouped(PLE_DIM, g, block_offset=layer * group_tiles[g]))
        operands.append(p)
    in_specs += [
        pl.BlockSpec((k, D_MODEL), lambda i: (0, 0), pipeline_mode=resident),
        pl.BlockSpec((PLE_DIM, D_MODEL), lambda i: (0, 0), pipeline_mode=resident),
        pl.BlockSpec((1, D_MODEL), lambda i: (0, 0)),
        pl.BlockSpec((D_MODEL, D_MODEL), lambda i: (0, 0), pipeline_mode=resident),
    ]
    operands += [w_out, w_proj, gain, w_gate]
    kernel = functools.partial(_out_pe_kernel, n_parts=n_parts, group_tiles=group_tiles,
                               per_group=per_group)
    return pl.pallas_call(
        kernel,
        name="out_pe",
        grid=(total_rows // tm,),
        in_specs=in_specs,
        out_specs=combined(D_MODEL, offset=0),
        out_shape=jax.ShapeDtypeStruct((total_rows, D_MODEL), F32),
        compiler_params=pltpu.CompilerParams(
            dimension_semantics=("arbitrary",),
            vmem_limit_bytes=VMEM_LIMIT_BYTES),
    )(*operands)


def _attn_front_kernel(x_ref, gain_ref, w_ref, cos_ref, sin_ref, qg_ref, kg_ref,
                       q_ref, k_ref, v_ref, zs_ref, hn_ref):
    tm = x_ref.shape[0]
    half = ROPE_DIM // 2
    tn = ATTN_FRONT_TN
    heads_per_block = tn // HEAD_DIM

    for r0 in range(0, tm, ATTN_FRONT_SUB):
        rows = slice(r0, r0 + ATTN_FRONT_SUB)
        x = x_ref[rows, :]
        hn_ref[rows, :] = (x * _rms_scale(x) * gain_ref[...]).astype(BF16)
        cos = cos_ref[:, rows].T
        sin = sin_ref[:, rows].T
        first_half = lax.broadcasted_iota(jnp.int32, cos.shape, 1) < half

        def norm_rope(slab, head_gain, cos=cos, sin=sin, first_half=first_half):
            s = slab * _rms_scale(slab) * head_gain
            partner = jnp.where(first_half, pltpu.roll(s, HEAD_DIM - half, axis=1),
                                pltpu.roll(s, half, axis=1))
            return s * cos + partner * sin

        for n in range(ATTN_IN_WIDTH // tn):
            y = _dot(hn_ref[rows, :], w_ref[:, n * tn:(n + 1) * tn])
            if n < Q_WIDTH // tn:
                for h in range(heads_per_block):
                    cols = slice(h * HEAD_DIM, (h + 1) * HEAD_DIM)
                    q_ref[rows, n * tn + h * HEAD_DIM:n * tn + (h + 1) * HEAD_DIM] = (
                        norm_rope(y[:, cols], qg_ref[...]).astype(BF16))
            elif n == Q_WIDTH // tn:
                for h in range(N_KV_HEADS):
                    cols = slice(h * HEAD_DIM, (h + 1) * HEAD_DIM)
                    k_ref[rows, cols] = norm_rope(y[:, cols], kg_ref[...]).astype(BF16)
                v_ref[rows, :] = y[:, KV_WIDTH:].astype(BF16)
            else:
                z0 = (n - Q_WIDTH // tn - 1) * tn
                zs_ref[rows, z0:z0 + tn] = _silu(y)


def _attn_front(x2d, gain, w_in, cos_t, sin_t, q_gain, k_gain, *, group_rows, group_seqs):
    rows = x2d.shape[0]
    tm = ATTN_FRONT_TM
    assert 2 * KV_WIDTH == ATTN_FRONT_TN

    def table_map(i):
        t0, _ = _tile_position(i * tm, group_rows, group_seqs)
        return (0, t0 // tm)

    table_spec = pl.BlockSpec((HEAD_DIM, tm), table_map)
    row_spec = lambda width: pl.BlockSpec((tm, width), lambda i: (i, 0))
    return pl.pallas_call(
        _attn_front_kernel,
        name="attn_front",
        grid=(rows // tm,),
        in_specs=[
            row_spec(D_MODEL),
            pl.BlockSpec((1, D_MODEL), lambda i: (0, 0)),
            pl.BlockSpec((D_MODEL, ATTN_IN_WIDTH), lambda i: (0, 0), pipeline_mode=pl.Buffered(1)),
            table_spec, table_spec,
            pl.BlockSpec((1, HEAD_DIM), lambda i: (0, 0)),
            pl.BlockSpec((1, HEAD_DIM), lambda i: (0, 0)),
        ],
        out_specs=[row_spec(Q_WIDTH), row_spec(KV_WIDTH), row_spec(KV_WIDTH), row_spec(Q_WIDTH)],
        out_shape=[
            jax.ShapeDtypeStruct((rows, Q_WIDTH), BF16),
            jax.ShapeDtypeStruct((rows, KV_WIDTH), BF16),
            jax.ShapeDtypeStruct((rows, KV_WIDTH), BF16),
            jax.ShapeDtypeStruct((rows, Q_WIDTH), F32),
        ],
        scratch_shapes=[pltpu.VMEM((tm, D_MODEL), BF16)],
        compiler_params=pltpu.CompilerParams(
            dimension_semantics=("arbitrary",),
            vmem_limit_bytes=VMEM_LIMIT_BYTES),
    )(x2d, gain, w_in, cos_t, sin_t, q_gain, k_gain)


def _attn_core_kernel(sink_ref, q_ref, kprev_ref, kmid_ref, knext_ref, vprev_ref, vmid_ref,
                      vnext_ref, zs_ref, o_ref, bias_ref, *, tm, group_rows, group_seqs):
    i = pl.program_id(0)
    blk = ATTN_BLOCK
    n_qb = tm // blk
    t0, seq = _tile_position(i * tm, group_rows, group_seqs)

    def window(prev_ref, mid_ref, next_ref, qb, cols):
        if qb == 0:
            return jnp.concatenate([prev_ref[:, cols], mid_ref[0:2 * blk, cols]], axis=0)
        if qb == n_qb - 1:
            return jnp.concatenate([mid_ref[tm - 2 * blk:tm, cols], next_ref[:, cols]], axis=0)
        return mid_ref[(qb - 1) * blk:(qb + 2) * blk, cols]

    ones = jnp.ones((3 * blk, HEAD_DIM), BF16)

    n_rows = GQA_GROUP * blk
    qi = lax.rem(lax.broadcasted_iota(jnp.int32, (n_rows, blk), 0), blk)
    kj = lax.broadcasted_iota(jnp.int32, (n_rows, blk), 1)
    prev_bias = jnp.where(kj >= qi, 0.0, NEG_BIG).astype(F32)
    next_bias = jnp.where(kj <= qi, 0.0, NEG_BIG).astype(F32)
    bias_ref[0] = jnp.where(t0 == 0, NEG_BIG, prev_bias)
    bias_ref[1] = prev_bias
    bias_ref[2] = jnp.where(t0 + tm == seq, NEG_BIG, next_bias)
    bias_ref[3] = next_bias

    sqrt_d = math.sqrt(HEAD_DIM)
    exp2_scale = LOG2_E / sqrt_d

    for qb in range(n_qb):
        row0 = qb * blk
        rows = pl.ds(row0, blk)
        b_prev = bias_ref[0 if qb == 0 else 1]
        b_next = bias_ref[2 if qb == n_qb - 1 else 3]
        for kvh in range(N_KV_HEADS):
            heads = [kvh * GQA_GROUP + hg for hg in range(GQA_GROUP)]
            qs = jnp.concatenate(
                [q_ref[rows, h * HEAD_DIM:(h + 1) * HEAD_DIM] for h in heads], axis=0)
            kv_cols = slice(kvh * HEAD_DIM, (kvh + 1) * HEAD_DIM)
            kw = window(kprev_ref, kmid_ref, knext_ref, qb, kv_cols)
            vw = jnp.concatenate([window(vprev_ref, vmid_ref, vnext_ref, qb, kv_cols), ones], axis=1)
            s = lax.dot_general(qs, kw, (((1,), (1,)), ((), ())), preferred_element_type=F32)
            sb = (s[:, 0:blk] + b_prev, s[:, blk:2 * blk], s[:, 2 * blk:] + b_next)
            row_max = jnp.max(jnp.maximum(jnp.maximum(sb[0], sb[1]), sb[2]), axis=-1, keepdims=True)
            e_rows, m_rows, sinks = [], [], []
            for hg, h in enumerate(heads):
                r = slice(hg * blk, (hg + 1) * blk)
                sink_raw = sink_ref[h] * sqrt_d
                m_h = jnp.maximum(row_max[r], sink_raw)
                e_rows.append(jnp.concatenate(
                    [jnp.exp2((b[r] - m_h) * exp2_scale).astype(BF16) for b in sb], axis=1))
                m_rows.append(m_h)
                sinks.append(sink_raw)
            e = jnp.concatenate(e_rows, axis=0)
            ov = _dot(e, vw)
            for hg, h in enumerate(heads):
                r = slice(hg * blk, (hg + 1) * blk)
                hcols = slice(h * HEAD_DIM, (h + 1) * HEAD_DIM)
                denom = ov[r, HEAD_DIM:] + jnp.exp2((sinks[hg] - m_rows[hg]) * exp2_scale)
                o_ref[rows, hcols] = (ov[r, :HEAD_DIM] * (1.0 / denom) * zs_ref[rows, hcols]
                                      ).astype(BF16)


def _attn_core(sink, q, k, v, zs, *, group_rows, group_seqs):
    rows = q.shape[0]
    tm = ATTN_CORE_TM
    blk = ATTN_BLOCK
    per = tm // blk
    n_blocks = rows // blk
    prev_map = lambda i: (jnp.maximum(i * per - 1, 0), 0)
    next_map = lambda i: (jnp.minimum((i + 1) * per, n_blocks - 1), 0)
    mid_map = lambda i: (i, 0)
    kernel = functools.partial(_attn_core_kernel, tm=tm, group_rows=group_rows,
                               group_seqs=group_seqs)
    return pl.pallas_call(
        kernel,
        name="attn_core",
        grid=(rows // tm,),
        in_specs=[
            pl.BlockSpec(memory_space=pltpu.SMEM),
            pl.BlockSpec((tm, Q_WIDTH), mid_map),
            pl.BlockSpec((blk, KV_WIDTH), prev_map),
            pl.BlockSpec((tm, KV_WIDTH), mid_map),
            pl.BlockSpec((blk, KV_WIDTH), next_map),
            pl.BlockSpec((blk, KV_WIDTH), prev_map),
            pl.BlockSpec((tm, KV_WIDTH), mid_map),
            pl.BlockSpec((blk, KV_WIDTH), next_map),
            pl.BlockSpec((tm, Q_WIDTH), mid_map),
        ],
        out_specs=pl.BlockSpec((tm, Q_WIDTH), mid_map),
        out_shape=jax.ShapeDtypeStruct((rows, Q_WIDTH), BF16),
        scratch_shapes=[pltpu.VMEM((4, GQA_GROUP * blk, blk), F32)],
        compiler_params=pltpu.CompilerParams(
            dimension_semantics=("arbitrary",),
            vmem_limit_bytes=VMEM_LIMIT_BYTES),
    )(sink, q, k, k, k, v, v, v, zs)


def _rope_tables(seq):
    freq = ROPE_THETA ** (-jnp.arange(0, ROPE_DIM, 2, dtype=F32) / ROPE_DIM)
    pos = jnp.arange(seq, dtype=F32)
    ang = freq[:, None] * pos[None, :]
    cos = jnp.cos(ang)
    sin = jnp.sin(ang)
    rest = HEAD_DIM - ROPE_DIM
    cos_t = jnp.concatenate([cos, cos, jnp.ones((rest, seq), F32)], axis=0)
    sin_t = jnp.concatenate([-sin, sin, jnp.zeros((rest, seq), F32)], axis=0)
    return cos_t, sin_t


def kernel(x_prompt, x_sample, p_prompt, p_sample, norm_g, a_w_in, a_w_grp, a_scale, a_w_out,
           b_w_in, b_q_norm, b_k_norm, b_sink, b_w_out, pe_w_proj, pe_norm_g, pe_w_gate):
    xs = (x_prompt, x_sample)
    group_seqs = tuple(x.shape[1] for x in xs)
    group_rows = tuple(x.shape[0] * x.shape[1] for x in xs)
    x2d = [x.reshape(rows, D_MODEL) for x, rows in zip(xs, group_rows)]
    p2d = [p.reshape(p.shape[0] * rows, PLE_DIM) for p, rows in zip((p_prompt, p_sample), group_rows)]

    a_w_ug, a_w_z = _fold_group_proj(a_w_in[0], a_w_grp[0])
    a_w_out_bf = a_w_out[0].astype(BF16)
    b_w_in_bf = b_w_in[0].astype(BF16)
    b_w_out_bf = b_w_out[0].astype(BF16)
    pe_w_proj_bf = pe_w_proj.astype(BF16)
    pe_w_gate_bf = pe_w_gate.astype(BF16)
    rope_cos, rope_sin = _rope_tables(max(group_seqs))

    t_sets = [[_pool_front(x, norm_g[0:1], a_w_ug, a_w_z, a_scale, seq=seq, part=part)
               for part in range(N_POOL_GROUPS // POOL_GROUPS_PER_CALL)]
              for x, seq in zip(x2d, group_seqs)]
    x1 = _out_pe(t_sets, a_w_out_bf, x2d, p2d, 0, pe_w_proj_bf[0], pe_norm_g[0:1], pe_w_gate_bf[0],
                 group_rows=group_rows)

    q, k, v, zs = _attn_front(x1, norm_g[1:2], b_w_in_bf, rope_cos, rope_sin, b_q_norm, b_k_norm,
                              group_rows=group_rows, group_seqs=group_seqs)
    og = _attn_core(b_sink[0], q, k, v, zs, group_rows=group_rows, group_seqs=group_seqs)
    ys = [_out_pe([[og]], b_w_out_bf, [x1], [p], 1, pe_w_proj_bf[1], pe_norm_g[1:2], pe_w_gate_bf[1],
                  group_rows=(rows,), combined_row_offset=sum(group_rows[:g]))
          for g, (p, rows) in enumerate(zip(p2d, group_rows))]
    return tuple(y.reshape(x.shape) for y, x in zip(ys, xs))
```
